```python
import jax, jax.numpy as jnp
from jax import lax
import numpy as np

D_MODEL = 1024
BATCH = 8
SEQ = 2048
DEPTH = 4
DEC_BATCH = 128
DEC_SEQ = 8
PAST_LEN = 16384
PAGE_SIZE = 128

CHUNK = 128
A_GROUPS = 4
A_WIDTH = 512
A_GC = A_WIDTH // A_GROUPS
B_WIDTH = 512
CONV_W = 31
C_WIDTH = 512
POOL_WINDOWS = (2, 4, 8, 16)
POOL_GROUPS = len(POOL_WINDOWS)
C_GC = C_WIDTH // POOL_GROUPS
POOL_BUF = max(POOL_WINDOWS) - 1
SSM_HEADS = 12
SSM_HEADDIM = 64
D_WIDTH = SSM_HEADS * SSM_HEADDIM
SSM_GROUPS = 4
SSM_STATE = 128
SSM_CONV = 4
SSD_CHUNK = 128
XBC_WIDTH = D_WIDTH + 2 * SSM_GROUPS * SSM_STATE
N_BRANCH = 4
IN_SPLITS = (A_WIDTH, A_WIDTH, B_WIDTH, B_WIDTH, C_WIDTH, D_WIDTH, XBC_WIDTH, SSM_HEADS, N_BRANCH * D_MODEL)
IN_WIDTH = sum(IN_SPLITS)
IN_SPLIT_IDX = [int(i) for i in np.cumsum(IN_SPLITS)[:-1]]
E_GROUPS = 4
E_PER_GROUP = 8
N_EXPERTS = E_GROUPS * E_PER_GROUP
TOP_K = 2
EXPERT_FF = 512
MOE_BLOCK = 128
DN_ALPHA = (2 * DEPTH) ** 0.25
DN_BETA = (8 * DEPTH) ** -0.25
LN_EPS = 1e-5

kernel_name = 'hybrid_gated_branch_decoder_step'


def _layer_norm(x, g, b):
    xf = x.astype(jnp.float32)
    mu = jnp.mean(xf, axis=-1, keepdims=True)
    var = jnp.mean(jnp.square(xf - mu), axis=-1, keepdims=True)
    y = (xf - mu) * lax.rsqrt(var + LN_EPS) * g.astype(jnp.float32) + b.astype(jnp.float32)
    return y.astype(x.dtype)


def _dw_conv(ext, w, bias):
    ch = ext.shape[-1]
    y = lax.conv_general_dilated(ext, w[:, None, :].astype(ext.dtype), window_strides=(1,), padding='VALID',
                                 dimension_numbers=('NWC', 'WIO', 'NWC'), feature_group_count=ch)
    return y + bias.astype(ext.dtype)


def _pad_seq(t, pad):
    return jnp.pad(t, [(0, 0), (0, pad)] + [(0, 0)] * (t.ndim - 2))


def _sgu_branch(u, v, ln_g, ln_b, w_s, b_s):
    bsz, L, _ = v.shape
    vn = _layer_norm(v, ln_g, ln_b)
    pad = (-L) % CHUNK
    vp = _pad_seq(vn, pad) if pad else vn
    nc = (L + pad) // CHUNK
    vc = vp.reshape(bsz, nc, CHUNK, A_GROUPS, A_GC)
    w = jnp.where(jnp.tril(jnp.ones((CHUNK, CHUNK), bool)), w_s, 0)
    s = jnp.einsum('gts,bnsgc->bntgc', w, vc) + b_s.T[:, :, None]
    s = s.reshape(bsz, nc * CHUNK, A_WIDTH)[:, :L]
    return u * s.astype(u.dtype), vn


def _conv_branch(a, g, conv_st, w, bias, ln_g, ln_b):
    h = a * jax.nn.sigmoid(g)
    ext = jnp.concatenate([conv_st.astype(h.dtype), h], axis=1)
    y = jax.nn.silu(_layer_norm(_dw_conv(ext, w, bias), ln_g, ln_b))
    return y, ext[:, -(CONV_W - 1):]


def _pool_branch(p, pool_st, start_pos, w_pool, scale):
    bsz, L, _ = p.shape
    ext = jnp.concatenate([pool_st.astype(p.dtype), p], axis=1)
    cs = jnp.pad(jnp.cumsum(ext.astype(jnp.float32), axis=1), ((0, 0), (1, 0), (0, 0)))
    pos = (start_pos + jnp.arange(L)).astype(jnp.float32)
    means = []
    for gi, win in enumerate(POOL_WINDOWS):
        sl = slice(gi * C_GC, (gi + 1) * C_GC)
        hi = cs[:, POOL_BUF + 1:POOL_BUF + 1 + L, sl]
        lo = cs[:, POOL_BUF + 1 - win:POOL_BUF + 1 - win + L, sl]
        cnt = jnp.minimum(float(win), pos + 1.0)
        means.append((hi - lo) / cnt[None, :, None])
    mixed = (jnp.concatenate(means, axis=-1) - p.astype(jnp.float32)).reshape(bsz, L, POOL_GROUPS, C_GC)
    out = jnp.einsum('blgc,gcd->blgd', mixed, w_pool.astype(jnp.float32)).reshape(bsz, L, C_WIDTH)
    out = out * scale.astype(jnp.float32)
    return out.astype(p.dtype), ext[:, -POOL_BUF:]


def _ssd_scan(x, dt, a, bm, cm, s0):
    bsz, L = x.shape[:2]
    pad = (-L) % SSD_CHUNK
    if pad:
        x, dt, bm, cm = _pad_seq(x, pad), _pad_seq(dt, pad), _pad_seq(bm, pad), _pad_seq(cm, pad)
    q = SSD_CHUNK
    nc = (L + pad) // q
    rep = SSM_HEADS // SSM_GROUPS
    bh = jnp.repeat(bm, rep, axis=2).reshape(bsz, nc, q, SSM_HEADS, SSM_STATE)
    ch = jnp.repeat(cm, rep, axis=2).reshape(bsz, nc, q, SSM_HEADS, SSM_STATE)
    xc = x.reshape(bsz, nc, q, SSM_HEADS, SSM_HEADDIM)
    dtc = dt.reshape(bsz, nc, q, SSM_HEADS)
    xd = xc * dtc[..., None]
    acs = jnp.cumsum(jnp.moveaxis(dtc * a, 3, 1), axis=-1)
    causal = jnp.tril(jnp.ones((q, q), bool))
    decay = jnp.exp(jnp.where(causal, acs[..., :, None] - acs[..., None, :], -jnp.inf))
    y_diag = jnp.einsum('bclhn,bcshn,bhcls,bcshp->bclhp', ch, bh, decay, xd)
    dstate = jnp.exp(acs[..., -1:] - acs)
    chunk_states = jnp.einsum('bclhn,bhcl,bclhp->bchpn', bh, dstate, xd)
    chunk_decay = jnp.exp(acs[..., -1])

    def step(s, inp):
        dec, st = inp
        return s * dec[:, :, None, None] + st, s

    s_fin, s_in = lax.scan(step, s0, (jnp.moveaxis(chunk_decay, 2, 0), jnp.moveaxis(chunk_states, 1, 0)))
    s_in = jnp.moveaxis(s_in, 0, 1)
    y_off = jnp.einsum('bclhn,bchpn,bhcl->bclhp', ch, s_in, jnp.exp(acs))
    y = (y_diag + y_off).reshape(bsz, nc * q, SSM_HEADS, SSM_HEADDIM)[:, :L]
    return y, s_fin


def _mamba_branch(z, xbc, dt_raw, conv_st, ssm_st, conv_w, conv_b, dt_bias, a_log, d_skip, norm_g):
    bsz, L, _ = z.shape
    f32 = jnp.float32
    ext = jnp.concatenate([conv_st.astype(xbc.dtype), xbc], axis=1)
    xbc_c = jax.nn.silu(_dw_conv(ext, conv_w, conv_b))
    xs, bm, cm = jnp.split(xbc_c, [D_WIDTH, D_WIDTH + SSM_GROUPS * SSM_STATE], axis=-1)
    xs = xs.reshape(bsz, L, SSM_HEADS, SSM_HEADDIM).astype(f32)
    bm = bm.reshape(bsz, L, SSM_GROUPS, SSM_STATE).astype(f32)
    cm = cm.reshape(bsz, L, SSM_GROUPS, SSM_STATE).astype(f32)
    dt = jax.nn.softplus(dt_raw.astype(f32) + dt_bias.astype(f32))
    a = -jnp.exp(a_log.astype(f32))
    y, s_new = _ssd_scan(xs, dt, a, bm, cm, ssm_st.astype(f32))
    y = y + d_skip.astype(f32)[:, None] * xs
    y = y.reshape(bsz, L, D_WIDTH) * jax.nn.silu(z.astype(f32))
    y = y * lax.rsqrt(jnp.mean(y * y, axis=-1, keepdims=True) + LN_EPS) * norm_g.astype(f32)
    return y.astype(z.dtype), ext[:, -(SSM_CONV - 1):], s_new.astype(ssm_st.dtype)


def _hier_moe(h, router_g, router_e, w_gate, w_up, w_down):
    bsz, L, dm = h.shape
    t = h.reshape(bsz * L, dm)
    n_tok = t.shape[0]
    tf = t.astype(jnp.float32)
    logit_g = tf @ router_g.astype(jnp.float32)
    grp = jnp.argmax(logit_g, axis=-1)
    p_grp = jnp.take_along_axis(jax.nn.softmax(logit_g, axis=-1), grp[:, None], axis=1)
    logit_e = (tf @ router_e.astype(jnp.float32)).reshape(n_tok, E_GROUPS, E_PER_GROUP)
    logit_e = jnp.take_along_axis(logit_e, grp[:, None, None], axis=1)[:, 0]
    top_v, top_i = lax.top_k(logit_e, TOP_K)
    w_slot = (jax.nn.softmax(top_v, axis=-1) * p_grp).reshape(-1)
    e_slot = (grp[:, None] * E_PER_GROUP + top_i).reshape(-1)
    n_slot = n_tok * TOP_K
    order = jnp.argsort(e_slot)
    e_sorted = e_slot[order]
    tok_sorted = order // TOP_K
    counts = jnp.bincount(e_slot, length=N_EXPERTS)
    padded = (counts + MOE_BLOCK - 1) // MOE_BLOCK * MOE_BLOCK
    end_pad = jnp.cumsum(padded)
    start_pad = end_pad - padded
    start_raw = jnp.cumsum(counts) - counts
    dest = start_pad[e_sorted] + jnp.arange(n_slot) - start_raw[e_sorted]
    n_blocks = -(-(n_slot + N_EXPERTS * (MOE_BLOCK - 1)) // MOE_BLOCK)
    buf = jnp.zeros((n_blocks * MOE_BLOCK, dm), h.dtype).at[dest].set(t[tok_sorted])
    block_e = jnp.minimum(jnp.searchsorted(end_pad, jnp.arange(n_blocks) * MOE_BLOCK, side='right'), N_EXPERTS - 1)

    def expert_block(args):
        xb, e = args
        hid = jax.nn.silu(xb @ w_gate[e]) * (xb @ w_up[e])
        return hid @ w_down[e]

    ybuf = lax.map(expert_block, (buf.reshape(n_blocks, MOE_BLOCK, dm), block_e)).reshape(-1, dm)
    y_slot = ybuf[dest].astype(jnp.float32) * w_slot[order][:, None]
    out = jnp.zeros((n_tok, dm), jnp.float32).at[tok_sorted].add(y_slot)
    return out.astype(h.dtype).reshape(bsz, L, dm)


def _trunk(x, c, conv_st, pool_st, sconv_st, ssm_st, start_pos, prm):
    new_conv, new_pool, new_sconv, new_ssm, new_v = [], [], [], [], []
    for l in range(DEPTH):
        ada = jax.nn.silu(c) @ prm['w_ada'][l] + prm['b_ada'][l]
        sh1, sc1, g1, sh2, sc2, g2 = jnp.split(ada, 6, axis=-1)
        xm = x * (1.0 + sc1[:, None, :]) + sh1[:, None, :]
        proj = xm @ prm['w_in'][l]
        u, v, ga, gg, pin, zz, xbc, dtr, gl = jnp.split(proj, IN_SPLIT_IDX, axis=-1)
        ya, vn = _sgu_branch(jax.nn.gelu(u), jax.nn.gelu(v), prm['sgu_ln_g'][l], prm['sgu_ln_b'][l],
                             prm['sgu_w'][l], prm['sgu_b'][l])
        yb, cst = _conv_branch(ga, gg, conv_st[l], prm['conv_w'][l], prm['conv_bias'][l],
                               prm['conv_ln_g'][l], prm['conv_ln_b'][l])
        yc, pst = _pool_branch(pin, pool_st[l], start_pos, prm['pool_w'][l], prm['pool_scale'][l])
        yd, scst, sst = _mamba_branch(zz, xbc, dtr, sconv_st[l], ssm_st[l], prm['ssm_conv_w'][l],
                                      prm['ssm_conv_b'][l], prm['ssm_dt_bias'][l], prm['ssm_a_log'][l],
                                      prm['ssm_d'][l], prm['ssm_norm_g'][l])
        gates = jax.nn.sigmoid(gl).reshape(gl.shape[:-1] + (N_BRANCH, D_MODEL))
        merged = (gates[..., 0, :] * (ya @ prm['w_br_a'][l]) + gates[..., 1, :] * (yb @ prm['w_br_b'][l])
                  + gates[..., 2, :] * (yc @ prm['w_br_c'][l]) + gates[..., 3, :] * (yd @ prm['w_br_d'][l]))
        mix = merged @ prm['w_o'][l]
        x = _layer_norm(DN_ALPHA * x + g1[:, None, :] * mix, prm['ln1_g'][l], prm['ln1_b'][l])
        xm2 = x * (1.0 + sc2[:, None, :]) + sh2[:, None, :]
        f = _hier_moe(xm2, prm['router_g'][l], prm['router_e'][l], prm['w_e_gate'][l], prm['w_e_up'][l],
                      prm['w_e_down'][l])
        x = _layer_norm(DN_ALPHA * x + g2[:, None, :] * f, prm['ln2_g'][l], prm['ln2_b'][l])
        new_conv.append(cst)
        new_pool.append(pst)
        new_sconv.append(scst)
        new_ssm.append(sst)
        new_v.append(vn)
    return (x, jnp.stack(new_conv), jnp.stack(new_pool), jnp.stack(new_sconv), jnp.stack(new_ssm),
            jnp.stack(new_v))


def setup_inputs(seed: int = 0) -> dict:
    key = jax.random.key(seed)
    ks = iter(jax.random.split(key, 64))

    def nrm(shape, scale=1.0):
        return jax.random.normal(next(ks), shape, jnp.float32) * scale

    def gain(shape):
        return 1.0 + nrm(shape, 0.02)

    d = D_MODEL
    dt = jnp.exp(jax.random.uniform(next(ks), (DEPTH, SSM_HEADS), jnp.float32, np.log(1e-3), np.log(1e-1)))
    inp = {}
    inp['x_prompt'] = nrm((BATCH, SEQ, d))
    inp['x_sample'] = nrm((DEC_BATCH, DEC_SEQ, d))
    inp['state_conv'] = nrm((DEPTH, DEC_BATCH, CONV_W - 1, B_WIDTH), 0.5)
    inp['state_pool'] = nrm((DEPTH, DEC_BATCH, POOL_BUF, C_WIDTH))
    inp['state_ssm_conv'] = nrm((DEPTH, DEC_BATCH, SSM_CONV - 1, XBC_WIDTH))
    inp['state_ssm'] = nrm((DEPTH, DEC_BATCH, SSM_HEADS, SSM_HEADDIM, SSM_STATE), 0.1)
    inp['c_prompt'] = nrm((BATCH, d))
    inp['c_sample'] = nrm((DEC_BATCH, d))
    inp['w_ada'] = nrm((DEPTH, d, 6 * d), 0.2 * d ** -0.5)
    inp['b_ada'] = nrm((DEPTH, 6 * d), 0.01)
    inp['w_in'] = nrm((DEPTH, d, IN_WIDTH), d ** -0.5)
    inp['sgu_ln_g'] = gain((DEPTH, A_WIDTH))
    inp['sgu_ln_b'] = nrm((DEPTH, A_WIDTH), 0.02)
    inp['sgu_w'] = nrm((DEPTH, A_GROUPS, CHUNK, CHUNK), CHUNK ** -0.5)
    inp['sgu_b'] = gain((DEPTH, A_GROUPS, CHUNK))
    inp['conv_w'] = nrm((DEPTH, CONV_W, B_WIDTH), CONV_W ** -0.5)
    inp['conv_bias'] = nrm((DEPTH, B_WIDTH), 0.02)
    inp['conv_ln_g'] = gain((DEPTH, B_WIDTH))
    inp['conv_ln_b'] = nrm((DEPTH, B_WIDTH), 0.02)
    inp['pool_w'] = nrm((DEPTH, POOL_GROUPS, C_GC, C_GC), C_GC ** -0.5)
    inp['pool_scale'] = gain((DEPTH, C_WIDTH))
    inp['ssm_conv_w'] = nrm((DEPTH, SSM_CONV, XBC_WIDTH), SSM_CONV ** -0.5)
    inp['ssm_conv_b'] = nrm((DEPTH, XBC_WIDTH), 0.02)
    inp['ssm_dt_bias'] = dt + jnp.log(-jnp.expm1(-dt))
    inp['ssm_a_log'] = jnp.log(jax.random.uniform(next(ks), (DEPTH, SSM_HEADS), jnp.float32, 1.0, 16.0))
    inp['ssm_d'] = gain((DEPTH, SSM_HEADS))
    inp['ssm_norm_g'] = gain((DEPTH, D_WIDTH))
    inp['w_br_a'] = nrm((DEPTH, A_WIDTH, d), A_WIDTH ** -0.5 * DN_BETA)
    inp['w_br_b'] = nrm((DEPTH, B_WIDTH, d), B_WIDTH ** -0.5 * DN_BETA)
    inp['w_br_c'] = nrm((DEPTH, C_WIDTH, d), C_WIDTH ** -0.5 * DN_BETA)
    inp['w_br_d'] = nrm((DEPTH, D_WIDTH, d), D_WIDTH ** -0.5 * DN_BETA)
    inp['w_o'] = nrm((DEPTH, d, d), d ** -0.5 * DN_BETA)
    inp['ln1_g'] = gain((DEPTH, d))
    inp['ln1_b'] = nrm((DEPTH, d), 0.02)
    inp['router_g'] = nrm((DEPTH, d, E_GROUPS), d ** -0.5)
    inp['router_e'] = nrm((DEPTH, d, N_EXPERTS), d ** -0.5)
    inp['w_e_gate'] = nrm((DEPTH, N_EXPERTS, d, EXPERT_FF), d ** -0.5)
    inp['w_e_up'] = nrm((DEPTH, N_EXPERTS, d, EXPERT_FF), d ** -0.5 * DN_BETA)
    inp['w_e_down'] = nrm((DEPTH, N_EXPERTS, EXPERT_FF, d), EXPERT_FF ** -0.5 * DN_BETA)
    inp['ln2_g'] = gain((DEPTH, d))
    inp['ln2_b'] = nrm((DEPTH, d), 0.02)
    return inp


def reference(x_prompt, x_sample, state_conv, state_pool, state_ssm_conv, state_ssm, c_prompt, c_sample,
              w_ada, b_ada, w_in, sgu_ln_g, sgu_ln_b, sgu_w, sgu_b, conv_w, conv_bias, conv_ln_g, conv_ln_b,
              pool_w, pool_scale, ssm_conv_w, ssm_conv_b, ssm_dt_bias, ssm_a_log, ssm_d, ssm_norm_g,
              w_br_a, w_br_b, w_br_c, w_br_d, w_o, ln1_g, ln1_b, router_g, router_e, w_e_gate, w_e_up,
              w_e_down, ln2_g, ln2_b):
    prm = dict(w_ada=w_ada, b_ada=b_ada, w_in=w_in, sgu_ln_g=sgu_ln_g, sgu_ln_b=sgu_ln_b, sgu_w=sgu_w,
               sgu_b=sgu_b, conv_w=conv_w, conv_bias=conv_bias, conv_ln_g=conv_ln_g, conv_ln_b=conv_ln_b,
               pool_w=pool_w, pool_scale=pool_scale, ssm_conv_w=ssm_conv_w, ssm_conv_b=ssm_conv_b,
               ssm_dt_bias=ssm_dt_bias, ssm_a_log=ssm_a_log, ssm_d=ssm_d, ssm_norm_g=ssm_norm_g,
               w_br_a=w_br_a, w_br_b=w_br_b, w_br_c=w_br_c, w_br_d=w_br_d, w_o=w_o, ln1_g=ln1_g,
               ln1_b=ln1_b, router_g=router_g, router_e=router_e, w_e_gate=w_e_gate, w_e_up=w_e_up,
               w_e_down=w_e_down, ln2_g=ln2_g, ln2_b=ln2_b)
    bp = x_prompt.shape[0]
    dtp = x_prompt.dtype
    z_conv = jnp.zeros((DEPTH, bp, CONV_W - 1, B_WIDTH), dtp)
    z_pool = jnp.zeros((DEPTH, bp, POOL_BUF, C_WIDTH), dtp)
    z_sconv = jnp.zeros((DEPTH, bp, SSM_CONV - 1, XBC_WIDTH), dtp)
    z_ssm = jnp.zeros((DEPTH, bp, SSM_HEADS, SSM_HEADDIM, SSM_STATE), jnp.float32)
    y_prompt, p_conv, p_pool, p_sconv, p_ssm, _ = _trunk(x_prompt, c_prompt, z_conv, z_pool, z_sconv, z_ssm, 0, prm)
    y_sample, s_conv, s_pool, s_sconv, s_ssm, s_v = _trunk(x_sample, c_sample, state_conv, state_pool,
                                                           state_ssm_conv, state_ssm, PAST_LEN, prm)
    return (y_prompt, y_sample, p_conv, p_pool, p_sconv, p_ssm, s_conv, s_pool, s_sconv, s_ssm, s_v)
```

```python
import functools
import math

import jax
import jax.numpy as jnp
import numpy as np
from jax import lax
from jax.experimental import pallas as pl
from jax.experimental.pallas import tpu as pltpu

D_MODEL = 1024
DEPTH = 4
A_WIDTH = 512
A_GROUPS = 4
B_WIDTH = 512
CONV_W = 31
C_WIDTH = 512
POOL_WINDOWS = (2, 4, 8, 16)
POOL_BUF = 15
SSM_HEADS = 12
SSM_HEADDIM = 64
D_WIDTH = SSM_HEADS * SSM_HEADDIM
SSM_GROUPS = 4
SSM_STATE = 128
SSM_CONV = 4
XBC_WIDTH = D_WIDTH + 2 * SSM_GROUPS * SSM_STATE
N_BRANCH = 4
E_PER_GROUP = 8
N_EXPERTS = 32
EXPERT_FF = 512
MOE_BLOCK = 128
DN_ALPHA = (2 * DEPTH) ** 0.25
LN_EPS = 1e-5
PAST_LEN = 16384
Y_WIDTH = A_WIDTH + B_WIDTH + C_WIDTH + D_WIDTH

OFF_U, OFF_V, OFF_GA, OFF_GG, OFF_P, OFF_Z, OFF_XBC, OFF_DT = 0, 512, 1024, 1536, 2048, 2560, 3328, 5120
WA_WIDTH = 5248
LANES = 128
HIST_CONV = 32
HIST_POOL = 16
HIST_SCONV = 8
VMEM_LIMIT = 56 * 1024 * 1024

f32 = jnp.float32
bf16 = jnp.bfloat16


def _dot(a, b):
    return jnp.dot(a, b, preferred_element_type=f32)


def _dot_nt(a, b):
    return lax.dot_general(a, b, (((1,), (1,)), ((), ())), preferred_element_type=f32)


def _dot_tn(a, b):
    return lax.dot_general(a, b, (((0,), (0,)), ((), ())), preferred_element_type=f32)


def _split3(x):
    h1 = x.astype(bf16)
    r1 = x - h1.astype(f32)
    h2 = r1.astype(bf16)
    r2 = r1 - h2.astype(f32)
    return h1, h2, r2.astype(bf16)


def _sigmoid(x):
    return 0.5 * (jnp.tanh(0.5 * x) + 1.0)


def _silu(x):
    return x * _sigmoid(x)


def _gelu(x):
    return 0.5 * x * (1.0 + jnp.tanh(math.sqrt(2.0 / math.pi) * (x + 0.044715 * (x * x * x))))


def _softplus(x):
    return jnp.maximum(x, 0.0) + jnp.log1p(jnp.exp(-jnp.abs(x)))


def _layer_norm(x, g, b):
    mu = jnp.mean(x, axis=-1, keepdims=True)
    xc = x - mu
    var = jnp.mean(xc * xc, axis=-1, keepdims=True)
    return xc * lax.rsqrt(var + LN_EPS) * g + b


def _ada_kernel(c_ref, w_ref, b_ref, o_ref):
    c = c_ref[...]
    a = _silu(c).astype(bf16)
    o_ref[0] = _dot(a, w_ref[0].astype(bf16)) + b_ref[0]


def _ada_call(c_all, w_ada, b_ada):
    n = c_all.shape[0]
    nblk = 6
    return pl.pallas_call(
        _ada_kernel,
        grid=(DEPTH, nblk),
        in_specs=[
            pl.BlockSpec((n, D_MODEL), lambda l, j: (0, 0)),
            pl.BlockSpec((1, D_MODEL, D_MODEL), lambda l, j: (l, 0, j)),
            pl.BlockSpec((1, 1, D_MODEL), lambda l, j: (l, 0, j)),
        ],
        out_specs=pl.BlockSpec((1, n, D_MODEL), lambda l, j: (l, 0, j)),
        out_shape=jax.ShapeDtypeStruct((DEPTH, n, 6 * D_MODEL), f32),
        compiler_params=pltpu.CompilerParams(dimension_semantics=("arbitrary", "arbitrary"),
                                             vmem_limit_bytes=VMEM_LIMIT),
    )(c_all, w_ada, b_ada.reshape(DEPTH, 1, 6 * D_MODEL))


def _mixer_kernel(x_ref, mod_ref, stc_ref, stp_ref, sts_ref, stm_ref,
                  wa_ref, sgw_ref, sgb_ref, slg_ref, slb_ref,
                  cw_ref, cb_ref, clg_ref, clb_ref,
                  pw_ref, ps_ref,
                  scw_ref, scb_ref, dtb_ref, alog_ref, dsk_ref, ng_ref,
                  y_ref, nc_ref, np_ref, ns_ref, nm_ref, vn_ref,
                  hbuf, pbuf, xbuf, sst, cbuf,
                  *, bb, lb, start_pos):
    r = bb * lb
    nb, q = bb, lb
    lq = q.bit_length() - 1
    j = pl.program_id(1)

    @pl.when(j == 0)
    def _():
        hbuf[:, HIST_CONV - (CONV_W - 1):HIST_CONV, :] = stc_ref[...]
        pbuf[:, HIST_POOL - POOL_BUF:HIST_POOL, :] = stp_ref[...]
        xbuf[:, HIST_SCONV - (SSM_CONV - 1):HIST_SCONV, :] = sts_ref[...]
        sst[...] = stm_ref[...]

    x = x_ref[...]
    xm = (x * (1.0 + mod_ref[:, 1:2, :]) + mod_ref[:, 0:1, :]).reshape(r, D_MODEL).astype(bf16)

    row_i = lax.broadcasted_iota(jnp.int32, (r, r), 0)
    col_i = lax.broadcasted_iota(jnp.int32, (r, r), 1)
    same = (row_i >> lq) == (col_i >> lq)
    causal = same & ((col_i & (q - 1)) <= (row_i & (q - 1)))
    causal_t = same & ((row_i & (q - 1)) <= (col_i & (q - 1)))

    u = _gelu(_dot(xm, wa_ref[:, OFF_U:OFF_U + A_WIDTH]))
    v = _gelu(_dot(xm, wa_ref[:, OFF_V:OFF_V + A_WIDTH]))
    vn = _layer_norm(v, slg_ref[...], slb_ref[...])
    vn_ref[...] = vn.reshape(bb, lb, A_WIDTH)
    vnb = vn.astype(bf16)
    for g in range(A_GROUPS):
        sl = slice(g * LANES, (g + 1) * LANES)
        wmix = jnp.where(causal, sgw_ref[g], 0.0).astype(bf16)
        s = _dot(wmix, vnb[:, sl]) + sgb_ref[:, sl]
        y_ref[:, sl] = (u[:, sl] * s).astype(bf16)

    ga = _dot(xm, wa_ref[:, OFF_GA:OFF_GA + B_WIDTH])
    gg = _dot(xm, wa_ref[:, OFF_GG:OFF_GG + B_WIDTH])
    h = ga * _sigmoid(gg)
    hbuf[:, HIST_CONV:HIST_CONV + lb, :] = h.reshape(bb, lb, B_WIDTH)
    base = HIST_CONV - (CONV_W - 1)
    lc = min(lb, 64)
    for rc in range(lb // lc):
        acc = jnp.zeros((bb, lc, B_WIDTH), f32) + cb_ref[...]
        for k in range(CONV_W):
            o = base + k + rc * lc
            acc = acc + hbuf[:, o:o + lc, :] * cw_ref[k:k + 1, :]
        cbuf[:, rc * lc:(rc + 1) * lc, :] = acc
    yb = _silu(_layer_norm(cbuf[...].reshape(r, B_WIDTH), clg_ref[...], clb_ref[...]))
    y_ref[:, A_WIDTH:A_WIDTH + B_WIDTH] = yb.astype(bf16)
    new_c = hbuf[:, lb + base:lb + HIST_CONV, :]
    nc_ref[...] = new_c
    hbuf[:, base:HIST_CONV, :] = new_c

    pin = _dot(xm, wa_ref[:, OFF_P:OFF_P + C_WIDTH])
    pbuf[:, HIST_POOL:HIST_POOL + lb, :] = pin.reshape(bb, lb, C_WIDTH)
    pos = (start_pos + j * lb + lax.broadcasted_iota(jnp.int32, (bb, lb, LANES), 1)).astype(f32)
    for gi, win in enumerate(POOL_WINDOWS):
        sl = slice(gi * LANES, (gi + 1) * LANES)
        acc = pbuf[:, HIST_POOL:HIST_POOL + lb, sl]
        for i in range(1, win):
            acc = acc + pbuf[:, HIST_POOL - i:HIST_POOL - i + lb, sl]
        cnt = jnp.minimum(float(win), pos + 1.0)
        mixed = (acc / cnt - pbuf[:, HIST_POOL:HIST_POOL + lb, sl]).reshape(r, LANES)
        out = _dot(mixed.astype(bf16), pw_ref[gi].astype(bf16)) * ps_ref[:, sl]
        y_ref[:, A_WIDTH + B_WIDTH + gi * LANES:A_WIDTH + B_WIDTH + (gi + 1) * LANES] = out.astype(bf16)
    new_p = pbuf[:, lb + HIST_POOL - POOL_BUF:lb + HIST_POOL, :]
    np_ref[...] = new_p
    pbuf[:, HIST_POOL - POOL_BUF:HIST_POOL, :] = new_p

    z = _dot(xm, wa_ref[:, OFF_Z:OFF_Z + D_WIDTH])
    xbc = _dot(xm, wa_ref[:, OFF_XBC:OFF_XBC + XBC_WIDTH])
    dtr = _dot(xm, wa_ref[:, OFF_DT:OFF_DT + LANES])
    xbuf[:, HIST_SCONV:HIST_SCONV + lb, :] = xbc.reshape(bb, lb, XBC_WIDTH)
    sbase = HIST_SCONV - (SSM_CONV - 1)
    acc = jnp.zeros((bb, lb, XBC_WIDTH), f32) + scb_ref[...]
    for k in range(SSM_CONV):
        acc = acc + xbuf[:, sbase + k:sbase + k + lb, :] * scw_ref[k:k + 1, :]
    xc = _silu(acc).reshape(r, XBC_WIDTH)
    new_s = xbuf[:, lb + sbase:lb + HIST_SCONV, :]
    ns_ref[...] = new_s
    xbuf[:, sbase:HIST_SCONV, :] = new_s

    dt = _softplus(dtr + dtb_ref[...])
    da = dt * (-jnp.exp(alog_ref[...]))
    d1, d2, d3 = _split3(da)
    mc = jnp.where(causal, 1.0, 0.0).astype(bf16)
    mct = jnp.where(causal_t, 1.0, 0.0).astype(bf16)
    ms = jnp.where(same, 1.0, 0.0).astype(bf16)
    acs = _dot(mc, d1) + _dot(mc, d2) + _dot(mc, d3)
    acs_t = _dot_tn(d1, mct) + _dot_tn(d2, mct) + _dot_tn(d3, mct)
    tot = _dot(ms, d1) + _dot(ms, d2) + _dot(ms, d3)

    lane = lax.broadcasted_iota(jnp.int32, (r, LANES), 1)
    lo_half = lane < SSM_HEADDIM
    cbs = []
    for g in range(SSM_GROUPS):
        bg = xc[:, D_WIDTH + g * SSM_STATE:D_WIDTH + (g + 1) * SSM_STATE].astype(bf16)
        cg = xc[:, D_WIDTH + (SSM_GROUPS + g) * SSM_STATE:D_WIDTH + (SSM_GROUPS + g + 1) * SSM_STATE].astype(bf16)
        cbs.append((bg, cg, _dot_nt(cg, bg)))
    rep = SSM_HEADS // SSM_GROUPS
    rowseq = lax.broadcasted_iota(jnp.int32, (r, LANES), 0) >> lq
    ysq = jnp.zeros((r, 1), f32)
    ypairs = []
    for i in range(SSM_HEADS // 2):
        h0, h1 = 2 * i, 2 * i + 1
        g0, g1 = h0 // rep, h1 // rep
        sl = slice(i * LANES, (i + 1) * LANES)
        xs = xc[:, sl]
        dtp = jnp.where(lo_half, dt[:, h0:h0 + 1], dt[:, h1:h1 + 1])
        acp = jnp.where(lo_half, acs[:, h0:h0 + 1], acs[:, h1:h1 + 1])
        totp = jnp.where(lo_half, tot[:, h0:h0 + 1], tot[:, h1:h1 + 1])
        xd = xs * dtp
        xdb = xd.astype(bf16)
        res = []
        for hh, gg_ in ((h0, g0), (h1, g1)):
            diff = acs[:, hh:hh + 1] - acs_t[hh:hh + 1, :]
            dec = jnp.exp(jnp.where(causal, diff, -1e30))
            res.append(_dot((cbs[gg_][2] * dec).astype(bf16), xdb))
        ydiag = jnp.where(lo_half, res[0], res[1])
        s_all = sst[:, i * LANES:(i + 1) * LANES, :].reshape(nb * LANES, SSM_STATE)
        s_bf = s_all.astype(bf16)
        full0 = _dot_nt(cbs[g0][1], s_bf)
        full1 = full0 if g1 == g0 else _dot_nt(cbs[g1][1], s_bf)
        yoff = jnp.zeros((r, LANES), f32)
        for b in range(nb):
            blk = jnp.where(lo_half, full0[:, b * LANES:(b + 1) * LANES], full1[:, b * LANES:(b + 1) * LANES])
            yoff = yoff + (jnp.where(rowseq == b, blk, 0.0) if nb > 1 else blk)
        yp = ydiag + yoff * jnp.exp(acp) + dsk_ref[:, sl] * xs
        yp = yp * _silu(z[:, sl])
        ysq = ysq + jnp.sum(yp * yp, axis=-1, keepdims=True)
        ypairs.append(yp)
        xdd = xd * jnp.exp(totp - acp)
        if nb > 1:
            colblk = lax.broadcasted_iota(jnp.int32, (r, nb * LANES), 1) >> 7
            rowblk = lax.broadcasted_iota(jnp.int32, (r, nb * LANES), 0) >> lq
            xblk = jnp.where(colblk == rowblk, jnp.concatenate([xdd] * nb, axis=1), 0.0).astype(bf16)
        else:
            xblk = xdd.astype(bf16)
        upd0 = _dot_tn(xblk, cbs[g0][0])
        if g1 != g0:
            upd1 = _dot_tn(xblk, cbs[g1][0])
            prow = lax.broadcasted_iota(jnp.int32, (nb * LANES, SSM_STATE), 0) & (LANES - 1)
            upd0 = jnp.where(prow < SSM_HEADDIM, upd0, upd1)
        upd0 = upd0.reshape(nb, LANES, SSM_STATE)
        for b in range(nb):
            for hh, half in ((h0, 0), (h1, 1)):
                fac = jnp.exp(jnp.broadcast_to(tot[b * q:b * q + 1, hh:hh + 1], (SSM_HEADDIM, SSM_STATE)))
                rs = slice(i * LANES + half * SSM_HEADDIM, i * LANES + (half + 1) * SSM_HEADDIM)
                sst[b, rs, :] = fac * sst[b, rs, :] + upd0[b, half * SSM_HEADDIM:(half + 1) * SSM_HEADDIM, :]
    rms = lax.rsqrt(ysq * (1.0 / D_WIDTH) + LN_EPS)
    for i in range(SSM_HEADS // 2):
        sl = slice(i * LANES, (i + 1) * LANES)
        yd = ypairs[i] * rms * ng_ref[:, sl]
        o = A_WIDTH + B_WIDTH + C_WIDTH + i * LANES
        y_ref[:, o:o + LANES] = yd.astype(bf16)
    nm_ref[...] = sst[...]


def _mixer_call(x, mods, st_conv, st_pool, st_sconv, st_ssm, wl, *, bb, lb, start_pos):
    b, l, _ = x.shape
    r = bb * lb
    grid = (b // bb, l // lb)
    kern = functools.partial(_mixer_kernel, bb=bb, lb=lb, start_pos=start_pos)

    def seq_spec(shape):
        return pl.BlockSpec((bb,) + shape, lambda i, j: (i,) + (0,) * len(shape))

    def const_spec(a):
        nd = a.ndim
        return pl.BlockSpec(a.shape, lambda i, j: (0,) * nd)

    weights = [wl["wa"], wl["sgu_w"], wl["sgu_bias_rows"], wl["sgu_ln_g"], wl["sgu_ln_b"],
               wl["conv_w"], wl["conv_bias"], wl["conv_ln_g"], wl["conv_ln_b"],
               wl["pool_w"], wl["pool_scale"],
               wl["ssm_conv_w"], wl["ssm_conv_b"], wl["dt_bias"], wl["a_log"], wl["d_skip"], wl["norm_g"]]
    in_specs = [
        pl.BlockSpec((bb, lb, D_MODEL), lambda i, j: (i, j, 0)),
        seq_spec((6, D_MODEL)),
        seq_spec((CONV_W - 1, B_WIDTH)),
        seq_spec((POOL_BUF, C_WIDTH)),
        seq_spec((SSM_CONV - 1, XBC_WIDTH)),
        seq_spec((D_WIDTH, SSM_STATE)),
    ] + [const_spec(a) for a in weights]
    nl = l // lb
    out_shape = (
        jax.ShapeDtypeStruct((b * l, Y_WIDTH), bf16),
        jax.ShapeDtypeStruct((b, CONV_W - 1, B_WIDTH), f32),
        jax.ShapeDtypeStruct((b, POOL_BUF, C_WIDTH), f32),
        jax.ShapeDtypeStruct((b, SSM_CONV - 1, XBC_WIDTH), f32),
        jax.ShapeDtypeStruct((b, D_WIDTH, SSM_STATE), f32),
        jax.ShapeDtypeStruct((b, l, A_WIDTH), f32),
    )
    out_specs = (
        pl.BlockSpec((r, Y_WIDTH), lambda i, j: (i * nl + j, 0)),
        seq_spec((CONV_W - 1, B_WIDTH)),
        seq_spec((POOL_BUF, C_WIDTH)),
        seq_spec((SSM_CONV - 1, XBC_WIDTH)),
        seq_spec((D_WIDTH, SSM_STATE)),
        pl.BlockSpec((bb, lb, A_WIDTH), lambda i, j: (i, j, 0)),
    )
    scratch = [
        pltpu.VMEM((bb, HIST_CONV + lb, B_WIDTH), f32),
        pltpu.VMEM((bb, HIST_POOL + lb, C_WIDTH), f32),
        pltpu.VMEM((bb, HIST_SCONV + lb, XBC_WIDTH), f32),
        pltpu.VMEM((bb, D_WIDTH, SSM_STATE), f32),
        pltpu.VMEM((bb, lb, B_WIDTH), f32),
    ]
    return pl.pallas_call(
        kern, grid=grid, in_specs=in_specs, out_specs=out_specs, out_shape=out_shape,
        scratch_shapes=scratch,
        compiler_params=pltpu.CompilerParams(dimension_semantics=("arbitrary", "arbitrary"),
                                             vmem_limit_bytes=VMEM_LIMIT),
    )(x, mods, st_conv, st_pool, st_sconv, st_ssm, *weights)


def _pack_bf16_pair(a, b):
    ua = lax.bitcast_convert_type(a.astype(bf16).astype(f32), jnp.uint32)
    ub = lax.bitcast_convert_type(b.astype(bf16).astype(f32), jnp.uint32)
    return (ua & jnp.uint32(0xFFFF0000)) | (ub >> 16)


def _unpack_bf16_pair(u):
    hi = lax.bitcast_convert_type(u & jnp.uint32(0xFFFF0000), f32)
    lo = lax.bitcast_convert_type(u << 16, f32)
    return hi, lo


def _merge_kernel(x_ref, mod_ref, y_ref, wg_ref, wbr_ref, wo_ref, lg_ref, lb_ref, wr_ref,
                  x1_ref, xpk_ref, rt_ref, *, bb, lb):
    r = bb * lb
    x = x_ref[...]
    xm = (x * (1.0 + mod_ref[:, 1:2, :]) + mod_ref[:, 0:1, :]).reshape(r, D_MODEL).astype(bf16)
    y = y_ref[...]
    offs = (0, A_WIDTH, A_WIDTH + B_WIDTH, A_WIDTH + B_WIDTH + C_WIDTH, Y_WIDTH)
    merged = jnp.zeros((r, D_MODEL), f32)
    for k in range(N_BRANCH):
        gate = _sigmoid(_dot(xm, wg_ref[:, k * D_MODEL:(k + 1) * D_MODEL]))
        merged = merged + gate * _dot(y[:, offs[k]:offs[k + 1]], wbr_ref[offs[k]:offs[k + 1], :])
    mix = _dot(merged.astype(bf16), wo_ref[...]).reshape(bb, lb, D_MODEL)
    x1 = _layer_norm(DN_ALPHA * x + mod_ref[:, 2:3, :] * mix, lg_ref[...], lb_ref[...])
    x1_ref[...] = x1
    xm2 = (x1 * (1.0 + mod_ref[:, 4:5, :]) + mod_ref[:, 3:4, :]).reshape(r, D_MODEL)
    half = D_MODEL // 2
    xpk_ref[...] = _pack_bf16_pair(xm2[:, :half], xm2[:, half:])

    a1, a2, _ = _split3(xm2)
    lg = _dot(a1, wr_ref[0]) + (_dot(a2, wr_ref[0]) + _dot(a1, wr_ref[1]))
    lane = lax.broadcasted_iota(jnp.int32, (r, LANES), 1).astype(f32)
    neg, big = -1e30, 1e6
    mask_g = lane < 4.0
    lgg = jnp.where(mask_g, lg, neg)
    mg = jnp.max(lgg, axis=-1, keepdims=True)
    grp = jnp.min(jnp.where(lgg == mg, lane, big), axis=-1, keepdims=True)
    p_grp = 1.0 / jnp.sum(jnp.where(mask_g, jnp.exp(lgg - mg), 0.0), axis=-1, keepdims=True)
    lo = 4.0 + grp * E_PER_GROUP
    mask_e = (lane >= lo) & (lane < lo + E_PER_GROUP)
    lge = jnp.where(mask_e, lg, neg)
    v1 = jnp.max(lge, axis=-1, keepdims=True)
    i1 = jnp.min(jnp.where(lge == v1, lane, big), axis=-1, keepdims=True)
    lge2 = jnp.where(lane == i1, neg, lge)
    v2 = jnp.max(lge2, axis=-1, keepdims=True)
    i2 = jnp.min(jnp.where(lge2 == v2, lane, big), axis=-1, keepdims=True)
    e21 = jnp.exp(v2 - v1)
    w1 = p_grp / (1.0 + e21)
    w2 = p_grp * e21 / (1.0 + e21)
    rt = jnp.where(lane == 0.0, i1 - 4.0,
                   jnp.where(lane == 1.0, i2 - 4.0,
                             jnp.where(lane == 2.0, w1, jnp.where(lane == 3.0, w2, 0.0))))
    rt_ref[...] = rt


def _merge_call(x, mods, ycat, wl, *, bb, lb):
    b, l, _ = x.shape
    r = bb * lb
    nl = l // lb
    grid = (b // bb, nl)
    kern = functools.partial(_merge_kernel, bb=bb, lb=lb)

    def const_spec(a):
        nd = a.ndim
        return pl.BlockSpec(a.shape, lambda i, j: (0,) * nd)

    weights = [wl["w_gl"], wl["w_br"], wl["w_o"], wl["ln1_g"], wl["ln1_b"], wl["w_router"]]
    n = b * l
    return pl.pallas_call(
        kern, grid=grid,
        in_specs=[pl.BlockSpec((bb, lb, D_MODEL), lambda i, j: (i, j, 0)),
                  pl.BlockSpec((bb, 6, D_MODEL), lambda i, j: (i, 0, 0)),
                  pl.BlockSpec((r, Y_WIDTH), lambda i, j: (i * nl + j, 0))] + [const_spec(a) for a in weights],
        out_specs=(pl.BlockSpec((bb, lb, D_MODEL), lambda i, j: (i, j, 0)),
                   pl.BlockSpec((r, D_MODEL // 2), lambda i, j: (i * nl + j, 0)),
                   pl.BlockSpec((r, LANES), lambda i, j: (i * nl + j, 0))),
        out_shape=(jax.ShapeDtypeStruct((b, l, D_MODEL), f32),
                   jax.ShapeDtypeStruct((n, D_MODEL // 2), jnp.uint32),
                   jax.ShapeDtypeStruct((n, LANES), f32)),
        compiler_params=pltpu.CompilerParams(dimension_semantics=("arbitrary", "arbitrary"),
                                             vmem_limit_bytes=VMEM_LIMIT),
    )(x, mods, ycat, *weights)


def _expert_kernel(be_ref, nv_ref, src_ref, x_ref, wg_ref, wu_ref, wd_ref, o_ref, xg):
    g = pl.program_id(0)
    b = pl.program_id(1)
    half = D_MODEL // 2

    @pl.when(b < nv_ref[g])
    def _():
        def body(i, c):
            t = src_ref[0, 0, i]
            xg[pl.ds(i, 1), :] = x_ref[pl.ds(t, 1), :]
            return c

        lax.fori_loop(0, MOE_BLOCK, body, 0, unroll=8)
        hi, lo = _unpack_bf16_pair(xg[...])
        hi = hi.astype(bf16)
        lo = lo.astype(bf16)
        hg = _dot(hi, wg_ref[0, :half, :]) + _dot(lo, wg_ref[0, half:, :])
        hu = _dot(hi, wu_ref[0, :half, :]) + _dot(lo, wu_ref[0, half:, :])
        hid = (_silu(hg) * hu).astype(bf16)
        y = _dot(hid, wd_ref[0])
        o_ref[...] = _pack_bf16_pair(y[:, :half], y[:, half:])

    @pl.when(b >= nv_ref[g])
    def _():
        o_ref[...] = jnp.zeros(o_ref.shape, o_ref.dtype)


def _expert_call(xpk, block_e, n_valid, src_tok, wl, *, n_groups, gsz, max_blocks):
    half = D_MODEL // 2
    grid_spec = pltpu.PrefetchScalarGridSpec(
        num_scalar_prefetch=2,
        grid=(n_groups, max_blocks),
        in_specs=[
            pl.BlockSpec((1, 1, MOE_BLOCK), lambda g, b, be, nv: (g * max_blocks + b, 0, 0),
                         memory_space=pltpu.SMEM),
            pl.BlockSpec((gsz, half), lambda g, b, be, nv: (g, 0)),
            pl.BlockSpec((1, D_MODEL, EXPERT_FF), lambda g, b, be, nv: (be[g * max_blocks + b], 0, 0)),
            pl.BlockSpec((1, D_MODEL, EXPERT_FF), lambda g, b, be, nv: (be[g * max_blocks + b], 0, 0)),
            pl.BlockSpec((1, EXPERT_FF, D_MODEL), lambda g, b, be, nv: (be[g * max_blocks + b], 0, 0)),
        ],
        out_specs=pl.BlockSpec((MOE_BLOCK, half), lambda g, b, be, nv: (g * max_blocks + b, 0)),
        scratch_shapes=[pltpu.VMEM((MOE_BLOCK, half), jnp.uint32)],
    )
    return pl.pallas_call(
        _expert_kernel, grid_spec=grid_spec,
        out_shape=jax.ShapeDtypeStruct((n_groups * max_blocks * MOE_BLOCK, half), jnp.uint32),
        compiler_params=pltpu.CompilerParams(dimension_semantics=("arbitrary", "arbitrary"),
                                             vmem_limit_bytes=VMEM_LIMIT),
    )(block_e, n_valid, src_tok.reshape(n_groups * max_blocks, 1, MOE_BLOCK), xpk,
      wl["w_e_gate"], wl["w_e_up"], wl["w_e_down"])


def _unsort_kernel(dst_ref, y_ref, o_ref, *, gsz):
    b = pl.program_id(1)

    @pl.when(b == 0)
    def _():
        o_ref[0, 2 * gsz:2 * gsz + 8, :] = jnp.zeros((8, D_MODEL // 2), jnp.uint32)

    def body(i, c):
        d = dst_ref[0, 0, i]
        o_ref[0, pl.ds(d, 1), :] = y_ref[pl.ds(i, 1), :]
        return c

    lax.fori_loop(0, MOE_BLOCK, body, 0, unroll=8)


def _unsort_call(ybuf, dst_row, *, n_groups, gsz, max_blocks):
    half = D_MODEL // 2
    return pl.pallas_call(
        functools.partial(_unsort_kernel, gsz=gsz),
        grid=(n_groups, max_blocks),
        in_specs=[
            pl.BlockSpec((1, 1, MOE_BLOCK), lambda g, b: (g * max_blocks + b, 0, 0), memory_space=pltpu.SMEM),
            pl.BlockSpec((MOE_BLOCK, half), lambda g, b: (g * max_blocks + b, 0)),
        ],
        out_specs=pl.BlockSpec((1, 2 * gsz + 8, half), lambda g, b: (g, 0, 0)),
        out_shape=jax.ShapeDtypeStruct((n_groups, 2 * gsz + 8, half), jnp.uint32),
        compiler_params=pltpu.CompilerParams(dimension_semantics=("arbitrary", "arbitrary"),
                                             vmem_limit_bytes=VMEM_LIMIT),
    )(dst_row.reshape(n_groups * max_blocks, 1, MOE_BLOCK), ybuf)


def _combine_kernel(x_ref, mod_ref, y0_ref, y1_ref, rt_ref, lg_ref, lb_ref, o_ref, *, bb, lb):
    r = bb * lb
    half = D_MODEL // 2
    x1 = x_ref[...].reshape(r, D_MODEL)
    rt = rt_ref[...]
    w0 = rt[:, 2:3]
    w1 = rt[:, 3:4]
    h0, l0 = _unpack_bf16_pair(y0_ref[0])
    h1, l1 = _unpack_bf16_pair(y1_ref[0])
    f = jnp.concatenate([w0 * h0 + w1 * h1, w0 * l0 + w1 * l1], axis=1).reshape(bb, lb, D_MODEL)
    t = DN_ALPHA * x_ref[...] + mod_ref[:, 5:6, :] * f
    del x1, half
    o_ref[...] = _layer_norm(t, lg_ref[...], lb_ref[...])


def _combine_call(x1, mods, ys, route, wl, *, bb, lb, gsz):
    b, l, _ = x1.shape
    r = bb * lb
    nl = l // lb
    half = D_MODEL // 2
    tiles_per_group = gsz // r

    def y_map(k):
        def m(i, j):
            t = i * nl + j
            return (t // tiles_per_group, k * tiles_per_group + t % tiles_per_group, 0)
        return m

    return pl.pallas_call(
        functools.partial(_combine_kernel, bb=bb, lb=lb),
        grid=(b // bb, nl),
        in_specs=[pl.BlockSpec((bb, lb, D_MODEL), lambda i, j: (i, j, 0)),
                  pl.BlockSpec((bb, 6, D_MODEL), lambda i, j: (i, 0, 0)),
                  pl.BlockSpec((1, r, half), y_map(0)),
                  pl.BlockSpec((1, r, half), y_map(1)),
                  pl.BlockSpec((r, LANES), lambda i, j: (i * nl + j, 0)),
                  pl.BlockSpec((1, D_MODEL), lambda i, j: (0, 0)),
                  pl.BlockSpec((1, D_MODEL), lambda i, j: (0, 0))],
        out_specs=pl.BlockSpec((bb, lb, D_MODEL), lambda i, j: (i, j, 0)),
        out_shape=jax.ShapeDtypeStruct((b, l, D_MODEL), f32),
        compiler_params=pltpu.CompilerParams(dimension_semantics=("arbitrary", "arbitrary"),
                                             vmem_limit_bytes=VMEM_LIMIT),
    )(x1, mods, ys, ys, route, wl["ln2_g"], wl["ln2_b"])


def _dispatch_tables(route, *, n_groups, gsz, max_blocks):
    e = route[:, :2].astype(jnp.int32).reshape(n_groups, 2 * gsz)
    oh = (e[:, :, None] == jnp.arange(N_EXPERTS, dtype=jnp.int32)).astype(jnp.int32)
    cum = jnp.cumsum(oh, axis=1)
    rank = jnp.sum(oh * cum, axis=-1) - 1
    counts = cum[:, -1, :]
    padded = (counts + MOE_BLOCK - 1) // MOE_BLOCK * MOE_BLOCK
    end_pad = jnp.cumsum(padded, axis=-1)
    start_pad = end_pad - padded
    dest = jnp.take_along_axis(start_pad, e, axis=1) + rank
    slot = jnp.arange(2 * gsz, dtype=jnp.int32)
    tok = slot // 2
    row = (slot % 2) * gsz + tok
    nrow = max_blocks * MOE_BLOCK
    gidx = jnp.arange(n_groups, dtype=jnp.int32)[:, None]
    src_tok = jnp.zeros((n_groups, nrow), jnp.int32).at[gidx, dest].set(jnp.broadcast_to(tok, dest.shape))
    dst_row = jnp.full((n_groups, nrow), 2 * gsz, jnp.int32).at[gidx, dest].set(jnp.broadcast_to(row, dest.shape))
    n_valid = (end_pad[:, -1] // MOE_BLOCK).astype(jnp.int32)
    blk_start = jnp.arange(max_blocks, dtype=jnp.int32) * MOE_BLOCK
    block_e = jnp.sum(blk_start[None, :, None] >= end_pad[:, None, :], axis=-1).astype(jnp.int32)
    block_e = jnp.minimum(block_e, N_EXPERTS - 1).reshape(n_groups * max_blocks)
    return block_e, n_valid, src_tok, dst_row


def _moe(x1, mods, xpk, route, wl, *, bb, lb, gsz):
    n = xpk.shape[0]
    n_groups = n // gsz
    max_blocks = (2 * gsz + N_EXPERTS * (MOE_BLOCK - 1)) // MOE_BLOCK
    block_e, n_valid, src_tok, dst_row = _dispatch_tables(route, n_groups=n_groups, gsz=gsz, max_blocks=max_blocks)
    ybuf = _expert_call(xpk, block_e, n_valid, src_tok, wl, n_groups=n_groups, gsz=gsz, max_blocks=max_blocks)
    ys = _unsort_call(ybuf, dst_row, n_groups=n_groups, gsz=gsz, max_blocks=max_blocks)
    return _combine_call(x1, mods, ys, route, wl, bb=bb, lb=lb, gsz=gsz)


def _expand_sgu_bias(sgu_b, q, r):
    idx = np.arange(r) % q
    rows = sgu_b[:, idx]
    return jnp.repeat(rows.T, LANES, axis=1)


def _expand_sgu_w(sgu_w, q, r):
    idx = np.arange(r) % q
    return sgu_w[:, idx[:, None], idx[None, :]]


def _prep_layer(p, l, q_r):
    w_in = p["w_in"][l]
    pad_dt = jnp.zeros((D_MODEL, LANES - SSM_HEADS), f32)
    gl_off = OFF_DT + SSM_HEADS
    wa = jnp.concatenate([w_in[:, :gl_off], pad_dt], axis=1).astype(bf16)
    w_gl = w_in[:, gl_off:].astype(bf16)
    wr = jnp.concatenate([p["router_g"][l], p["router_e"][l],
                          jnp.zeros((D_MODEL, LANES - 4 - N_EXPERTS), f32)], axis=1)
    wr_hi = wr.astype(bf16)
    wr_lo = (wr - wr_hi.astype(f32)).astype(bf16)
    lane_pad = lambda v: jnp.concatenate([v, jnp.zeros((LANES - v.shape[0],), f32)])[None, :]
    wl = dict(
        wa=wa, w_gl=w_gl,
        sgu_ln_g=p["sgu_ln_g"][l][None], sgu_ln_b=p["sgu_ln_b"][l][None],
        conv_w=p["conv_w"][l], conv_bias=p["conv_bias"][l][None],
        conv_ln_g=p["conv_ln_g"][l][None], conv_ln_b=p["conv_ln_b"][l][None],
        pool_w=p["pool_w"][l], pool_scale=p["pool_scale"][l][None],
        ssm_conv_w=p["ssm_conv_w"][l], ssm_conv_b=p["ssm_conv_b"][l][None],
        dt_bias=lane_pad(p["ssm_dt_bias"][l]), a_log=lane_pad(p["ssm_a_log"][l]),
        d_skip=jnp.repeat(p["ssm_d"][l], SSM_HEADDIM)[None], norm_g=p["ssm_norm_g"][l][None],
        w_br=jnp.concatenate([p["w_br_a"][l], p["w_br_b"][l], p["w_br_c"][l], p["w_br_d"][l]], axis=0).astype(bf16),
        w_o=p["w_o"][l].astype(bf16),
        ln1_g=p["ln1_g"][l][None], ln1_b=p["ln1_b"][l][None],
        w_router=jnp.stack([wr_hi, wr_lo]),
        w_e_gate=p["w_e_gate_bf"][l], w_e_up=p["w_e_up_bf"][l], w_e_down=p["w_e_down_bf"][l],
        ln2_g=p["ln2_g"][l][None], ln2_b=p["ln2_b"][l][None],
    )
    per_group = []
    for q, r in q_r:
        per_group.append(dict(sgu_w=_expand_sgu_w(p["sgu_w"][l], q, r),
                              sgu_bias_rows=_expand_sgu_bias(p["sgu_b"][l], q, r)))
    return wl, per_group


PROMPT_MIX = (1, 128)
SAMPLE_MIX = (8, 8)
PROMPT_TOK = (1, 256)
SAMPLE_TOK = (32, 8)
PROMPT_GROUP = 4096
SAMPLE_GROUP = 1024


def kernel(x_prompt, x_sample, state_conv, state_pool, state_ssm_conv, state_ssm, c_prompt, c_sample,
           w_ada, b_ada, w_in, sgu_ln_g, sgu_ln_b, sgu_w, sgu_b, conv_w, conv_bias, conv_ln_g, conv_ln_b,
           pool_w, pool_scale, ssm_conv_w, ssm_conv_b, ssm_dt_bias, ssm_a_log, ssm_d, ssm_norm_g,
           w_br_a, w_br_b, w_br_c, w_br_d, w_o, ln1_g, ln1_b, router_g, router_e, w_e_gate, w_e_up,
           w_e_down, ln2_g, ln2_b):
    return _forward(x_prompt, x_sample, state_conv, state_pool, state_ssm_conv, state_ssm, c_prompt, c_sample,
                    w_ada, b_ada, w_in, sgu_ln_g, sgu_ln_b, sgu_w, sgu_b, conv_w, conv_bias, conv_ln_g, conv_ln_b,
                    pool_w, pool_scale, ssm_conv_w, ssm_conv_b, ssm_dt_bias, ssm_a_log, ssm_d, ssm_norm_g,
                    w_br_a, w_br_b, w_br_c, w_br_d, w_o, ln1_g, ln1_b, router_g, router_e, w_e_gate, w_e_up,
                    w_e_down, ln2_g, ln2_b)


def _forward(x_prompt, x_sample, state_conv, state_pool, state_ssm_conv, state_ssm, c_prompt, c_sample,
             w_ada, b_ada, w_in, sgu_ln_g, sgu_ln_b, sgu_w, sgu_b, conv_w, conv_bias, conv_ln_g, conv_ln_b,
             pool_w, pool_scale, ssm_conv_w, ssm_conv_b, ssm_dt_bias, ssm_a_log, ssm_d, ssm_norm_g,
             w_br_a, w_br_b, w_br_c, w_br_d, w_o, ln1_g, ln1_b, router_g, router_e, w_e_gate, w_e_up,
             w_e_down, ln2_g, ln2_b, prompt_group=PROMPT_GROUP, sample_group=SAMPLE_GROUP,
             sample_tok=SAMPLE_TOK):
    p = dict(w_in=w_in, sgu_ln_g=sgu_ln_g, sgu_ln_b=sgu_ln_b, sgu_w=sgu_w, sgu_b=sgu_b, conv_w=conv_w,
             conv_bias=conv_bias, conv_ln_g=conv_ln_g, conv_ln_b=conv_ln_b, pool_w=pool_w, pool_scale=pool_scale,
             ssm_conv_w=ssm_conv_w, ssm_conv_b=ssm_conv_b, ssm_dt_bias=ssm_dt_bias, ssm_a_log=ssm_a_log,
             ssm_d=ssm_d, ssm_norm_g=ssm_norm_g, w_br_a=w_br_a, w_br_b=w_br_b, w_br_c=w_br_c, w_br_d=w_br_d,
             w_o=w_o, ln1_g=ln1_g, ln1_b=ln1_b, router_g=router_g, router_e=router_e, ln2_g=ln2_g, ln2_b=ln2_b,
             w_e_gate_bf=w_e_gate.astype(bf16), w_e_up_bf=w_e_up.astype(bf16), w_e_down_bf=w_e_down.astype(bf16))
    bp = x_prompt.shape[0]
    bs = x_sample.shape[0]
    ada = _ada_call(jnp.concatenate([c_prompt, c_sample], axis=0), w_ada, b_ada)
    ada = ada.reshape(DEPTH, bp + bs, 6, D_MODEL)

    groups = [
        dict(x=x_prompt, mix=PROMPT_MIX, tok=PROMPT_TOK, gsz=prompt_group, start=0, sl=slice(0, bp),
             conv=jnp.zeros((DEPTH, bp, CONV_W - 1, B_WIDTH), f32),
             pool=jnp.zeros((DEPTH, bp, POOL_BUF, C_WIDTH), f32),
             sconv=jnp.zeros((DEPTH, bp, SSM_CONV - 1, XBC_WIDTH), f32),
             ssm=jnp.zeros((DEPTH, bp, D_WIDTH, SSM_STATE), f32)),
        dict(x=x_sample, mix=SAMPLE_MIX, tok=sample_tok, gsz=sample_group, start=PAST_LEN, sl=slice(bp, bp + bs),
             conv=state_conv, pool=state_pool, sconv=state_ssm_conv,
             ssm=state_ssm.reshape(DEPTH, bs, D_WIDTH, SSM_STATE)),
    ]
    outs = [dict(conv=[], pool=[], sconv=[], ssm=[], vn=[]) for _ in groups]
    q_r = [(g["mix"][1], g["mix"][0] * g["mix"][1]) for g in groups]
    for l in range(DEPTH):
        wl, per_group = _prep_layer(p, l, q_r)
        for gi, g in enumerate(groups):
            mods = ada[l, g["sl"]]
            wl_g = dict(wl, **per_group[gi])
            bb, lb = g["mix"]
            ycat, nc, npl, ns, nm, vn = _mixer_call(g["x"], mods, g["conv"][l], g["pool"][l], g["sconv"][l],
                                                    g["ssm"][l], wl_g, bb=bb, lb=lb, start_pos=g["start"])
            bb, lb = g["tok"]
            x1, xpk, route = _merge_call(g["x"], mods, ycat, wl_g, bb=bb, lb=lb)
            g["x"] = _moe(x1, mods, xpk, route, wl_g, bb=bb, lb=lb, gsz=g["gsz"])
            o = outs[gi]
            o["conv"].append(nc)
            o["pool"].append(npl)
            o["sconv"].append(ns)
            o["ssm"].append(nm)
            o["vn"].append(vn)

    def st(gi, name):
        return jnp.stack(outs[gi][name])

    def ssm_out(gi, b):
        return st(gi, "ssm").reshape(DEPTH, b, SSM_HEADS, SSM_HEADDIM, SSM_STATE)

    return (groups[0]["x"], groups[1]["x"],
            st(0, "conv"), st(0, "pool"), st(0, "sconv"), ssm_out(0, bp),
            st(1, "conv"), st(1, "pool"), st(1, "sconv"), ssm_out(1, bs),
            st(1, "vn"))
```

```python
import functools
import math

import jax
import jax.numpy as jnp
import numpy as np
from jax import lax
from jax.experimental import pallas as pl
from jax.experimental.pallas import tpu as pltpu

D_MODEL = 1024
DEPTH = 4
A_WIDTH = 512
A_GROUPS = 4
B_WIDTH = 512
CONV_W = 31
C_WIDTH = 512
POOL_WINDOWS = (2, 4, 8, 16)
POOL_BUF = 15
SSM_HEADS = 12
SSM_HEADDIM = 64
D_WIDTH = SSM_HEADS * SSM_HEADDIM
SSM_GROUPS = 4
SSM_STATE = 128
SSM_CONV = 4
XBC_WIDTH = D_WIDTH + 2 * SSM_GROUPS * SSM_STATE
N_BRANCH = 4
E_PER_GROUP = 8
N_EXPERTS = 32
EXPERT_FF = 512
MOE_BLOCK = 128
DN_ALPHA = (2 * DEPTH) ** 0.25
LN_EPS = 1e-5
PAST_LEN = 16384
Y_WIDTH = A_WIDTH + B_WIDTH + C_WIDTH + D_WIDTH
HALF = D_MODEL // 2

OFF_U, OFF_V, OFF_GA, OFF_GG, OFF_P, OFF_Z, OFF_XBC, OFF_DT = 0, 512, 1024, 1536, 2048, 2560, 3328, 5120
GATE_OFF = OFF_DT + SSM_HEADS
WA_WIDTH = 5248
LANES = 128
SUBLANES = 8
HIST_CONV = 32
HIST_POOL = 16
HIST_SCONV = 8
VMEM_LIMIT = 56 * 1024 * 1024
RANK_TILE = 256

f32 = jnp.float32
bf16 = jnp.bfloat16


def _dot(a, b):
    return jnp.dot(a, b, preferred_element_type=f32)


def _dot_nt(a, b):
    return lax.dot_general(a, b, (((1,), (1,)), ((), ())), preferred_element_type=f32)


def _dot_tn(a, b):
    return lax.dot_general(a, b, (((0,), (0,)), ((), ())), preferred_element_type=f32)


def _split3(x):
    h1 = x.astype(bf16)
    r1 = x - h1.astype(f32)
    h2 = r1.astype(bf16)
    r2 = r1 - h2.astype(f32)
    return h1, h2, r2.astype(bf16)


def _sigmoid(x):
    return 0.5 * (jnp.tanh(0.5 * x) + 1.0)


def _silu(x):
    return x * _sigmoid(x)


def _gelu(x):
    return 0.5 * x * (1.0 + jnp.tanh(math.sqrt(2.0 / math.pi) * (x + 0.044715 * (x * x * x))))


def _softplus(x):
    return jnp.maximum(x, 0.0) + jnp.log1p(jnp.exp(-jnp.abs(x)))


def _layer_norm(x, g, b):
    mu = jnp.mean(x, axis=-1, keepdims=True)
    xc = x - mu
    var = jnp.mean(xc * xc, axis=-1, keepdims=True)
    return xc * lax.rsqrt(var + LN_EPS) * g + b


def _layer_spec(a, l):
    nd = a.ndim
    return pl.BlockSpec((None,) + a.shape[1:], lambda *_: (l,) + (0,) * (nd - 1))


def _params(n_axes=2):
    return pltpu.CompilerParams(dimension_semantics=("arbitrary",) * n_axes, vmem_limit_bytes=VMEM_LIMIT)


def _ada_kernel(c_ref, w_ref, b_ref, o_ref):
    a = _silu(c_ref[...]).astype(bf16)
    o_ref[0] = _dot(a, w_ref[0].astype(bf16)) + b_ref[0]


def _ada_call(c_all, w_ada, b_ada):
    n = c_all.shape[0]
    return pl.pallas_call(
        _ada_kernel,
        grid=(DEPTH, 6),
        in_specs=[
            pl.BlockSpec((n, D_MODEL), lambda l, j: (0, 0)),
            pl.BlockSpec((1, D_MODEL, D_MODEL), lambda l, j: (l, 0, j)),
            pl.BlockSpec((1, 1, D_MODEL), lambda l, j: (l, 0, j)),
        ],
        out_specs=pl.BlockSpec((1, n, D_MODEL), lambda l, j: (l, 0, j)),
        out_shape=jax.ShapeDtypeStruct((DEPTH, n, 6 * D_MODEL), f32),
        compiler_params=_params(),
    )(c_all, w_ada, b_ada.reshape(DEPTH, 1, 6 * D_MODEL))


def _mixer_kernel(x_ref, mod_ref, stc_ref, stp_ref, sts_ref, stm_ref, *rest, bb, lb, start_pos, emit_vn):
    if emit_vn:
        rest = rest[1:]
    (wa_ref, sgw_ref, sgb_ref, slg_ref, slb_ref, cw_ref, cb_ref, clg_ref, clb_ref, pw_ref, ps_ref,
     scw_ref, scb_ref, dtb_ref, alog_ref, dsk_ref, ng_ref, y_ref, nc_ref, np_ref, ns_ref, nm_ref) = rest[:22]
    vn_ref = rest[22] if emit_vn else None
    hbuf, hsh, pbuf, xbuf, sst, cbuf = rest[-6:]
    r = bb * lb
    nb, q = bb, lb
    lq = q.bit_length() - 1
    j = pl.program_id(1)

    @pl.when(j == 0)
    def _():
        hbuf[:, HIST_CONV - (CONV_W - 1):HIST_CONV, :] = stc_ref[...]
        pbuf[:, HIST_POOL - POOL_BUF:HIST_POOL, :] = stp_ref[...]
        xbuf[:, HIST_SCONV - (SSM_CONV - 1):HIST_SCONV, :] = sts_ref[...]
        sst[...] = stm_ref[...]

    x = x_ref[...]
    xm = (x * (1.0 + mod_ref[:, 1:2, :]) + mod_ref[:, 0:1, :]).reshape(r, D_MODEL).astype(bf16)

    row_i = lax.broadcasted_iota(jnp.int32, (r, r), 0)
    col_i = lax.broadcasted_iota(jnp.int32, (r, r), 1)
    same = (row_i >> lq) == (col_i >> lq)
    causal = same & ((col_i & (q - 1)) <= (row_i & (q - 1)))
    causal_t = same & ((row_i & (q - 1)) <= (col_i & (q - 1)))

    u = _gelu(_dot(xm, wa_ref[:, OFF_U:OFF_U + A_WIDTH]))
    v = _gelu(_dot(xm, wa_ref[:, OFF_V:OFF_V + A_WIDTH]))
    vn = _layer_norm(v, slg_ref[...], slb_ref[...])
    if emit_vn:
        vn_ref[...] = vn.reshape(bb, lb, A_WIDTH)
    vnb = vn.astype(bf16)
    for g in range(A_GROUPS):
        sl = slice(g * LANES, (g + 1) * LANES)
        wmix = jnp.where(causal, sgw_ref[g], 0.0).astype(bf16)
        s = _dot(wmix, vnb[:, sl]) + sgb_ref[:, sl]
        y_ref[:, sl] = (u[:, sl] * s).astype(bf16)

    ga = _dot(xm, wa_ref[:, OFF_GA:OFF_GA + B_WIDTH])
    gg = _dot(xm, wa_ref[:, OFF_GG:OFF_GG + B_WIDTH])
    h = ga * _sigmoid(gg)
    hbuf[:, HIST_CONV:HIST_CONV + lb, :] = h.reshape(bb, lb, B_WIDTH)
    base = HIST_CONV - (CONV_W - 1)
    span = HIST_CONV + lb - SUBLANES
    for s in range(1, SUBLANES):
        hsh[s - 1, :, 0:span, :] = hbuf[:, s:s + span, :]
    lc = min(lb, 64)
    for rc in range(lb // lc):
        acc = jnp.zeros((bb, lc, B_WIDTH), f32) + cb_ref[...]
        for k in range(CONV_W):
            o = base + k + rc * lc
            s, a = o % SUBLANES, o - o % SUBLANES
            tap = hbuf[:, a:a + lc, :] if s == 0 else hsh[s - 1, :, a:a + lc, :]
            acc = acc + tap * cw_ref[k:k + 1, :]
        cbuf[:, rc * lc:(rc + 1) * lc, :] = acc
    yb = _silu(_layer_norm(cbuf[...].reshape(r, B_WIDTH), clg_ref[...], clb_ref[...]))
    y_ref[:, A_WIDTH:A_WIDTH + B_WIDTH] = yb.astype(bf16)
    new_c = hbuf[:, lb + base:lb + HIST_CONV, :]
    nc_ref[...] = new_c
    hbuf[:, base:HIST_CONV, :] = new_c

    pin = _dot(xm, wa_ref[:, OFF_P:OFF_P + C_WIDTH])
    pbuf[:, HIST_POOL:HIST_POOL + lb, :] = pin.reshape(bb, lb, C_WIDTH)
    pos = (start_pos + j * lb + lax.broadcasted_iota(jnp.int32, (bb, lb, LANES), 1)).astype(f32)
    for gi, win in enumerate(POOL_WINDOWS):
        sl = slice(gi * LANES, (gi + 1) * LANES)
        acc = pbuf[:, HIST_POOL:HIST_POOL + lb, sl]
        for i in range(1, win):
            acc = acc + pbuf[:, HIST_POOL - i:HIST_POOL - i + lb, sl]
        cnt = jnp.minimum(float(win), pos + 1.0)
        mixed = (acc / cnt - pbuf[:, HIST_POOL:HIST_POOL + lb, sl]).reshape(r, LANES)
        out = _dot(mixed.astype(bf16), pw_ref[gi].astype(bf16)) * ps_ref[:, sl]
        y_ref[:, A_WIDTH + B_WIDTH + gi * LANES:A_WIDTH + B_WIDTH + (gi + 1) * LANES] = out.astype(bf16)
    new_p = pbuf[:, lb + HIST_POOL - POOL_BUF:lb + HIST_POOL, :]
    np_ref[...] = new_p
    pbuf[:, HIST_POOL - POOL_BUF:HIST_POOL, :] = new_p

    z = _dot(xm, wa_ref[:, OFF_Z:OFF_Z + D_WIDTH])
    xbc = _dot(xm, wa_ref[:, OFF_XBC:OFF_XBC + XBC_WIDTH])
    dtr = _dot(xm, wa_ref[:, OFF_DT:OFF_DT + LANES])
    xbuf[:, HIST_SCONV:HIST_SCONV + lb, :] = xbc.reshape(bb, lb, XBC_WIDTH)
    sbase = HIST_SCONV - (SSM_CONV - 1)
    acc = jnp.zeros((bb, lb, XBC_WIDTH), f32) + scb_ref[...]
    for k in range(SSM_CONV):
        acc = acc + xbuf[:, sbase + k:sbase + k + lb, :] * scw_ref[k:k + 1, :]
    xc = _silu(acc).reshape(r, XBC_WIDTH)
    new_s = xbuf[:, lb + sbase:lb + HIST_SCONV, :]
    ns_ref[...] = new_s
    xbuf[:, sbase:HIST_SCONV, :] = new_s

    dt = _softplus(dtr + dtb_ref[...])
    da = dt * (-jnp.exp(alog_ref[...]))
    d1, d2, d3 = _split3(da)
    mc = jnp.where(causal, 1.0, 0.0).astype(bf16)
    mct = jnp.where(causal_t, 1.0, 0.0).astype(bf16)
    ms = jnp.where(same, 1.0, 0.0).astype(bf16)
    acs = _dot(mc, d1) + _dot(mc, d2) + _dot(mc, d3)
    acs_t = _dot_tn(d1, mct) + _dot_tn(d2, mct) + _dot_tn(d3, mct)
    tot = _dot(ms, d1) + _dot(ms, d2) + _dot(ms, d3)

    lane = lax.broadcasted_iota(jnp.int32, (r, LANES), 1)
    lo_half = lane < SSM_HEADDIM
    cbs = []
    for g in range(SSM_GROUPS):
        bg = xc[:, D_WIDTH + g * SSM_STATE:D_WIDTH + (g + 1) * SSM_STATE].astype(bf16)
        cg = xc[:, D_WIDTH + (SSM_GROUPS + g) * SSM_STATE:D_WIDTH + (SSM_GROUPS + g + 1) * SSM_STATE].astype(bf16)
        cbs.append((bg, cg, _dot_nt(cg, bg)))
    rep = SSM_HEADS // SSM_GROUPS
    rowseq = lax.broadcasted_iota(jnp.int32, (r, LANES), 0) >> lq
    ysq = jnp.zeros((r, 1), f32)
    ypairs = []
    for i in range(SSM_HEADS // 2):
        h0, h1 = 2 * i, 2 * i + 1
        g0, g1 = h0 // rep, h1 // rep
        sl = slice(i * LANES, (i + 1) * LANES)
        xs = xc[:, sl]
        dtp = jnp.where(lo_half, dt[:, h0:h0 + 1], dt[:, h1:h1 + 1])
        acp = jnp.where(lo_half, acs[:, h0:h0 + 1], acs[:, h1:h1 + 1])
        totp = jnp.where(lo_half, tot[:, h0:h0 + 1], tot[:, h1:h1 + 1])
        xd = xs * dtp
        xdb = xd.astype(bf16)
        res = []
        for hh, gg_ in ((h0, g0), (h1, g1)):
            diff = acs[:, hh:hh + 1] - acs_t[hh:hh + 1, :]
            dec = jnp.exp(jnp.where(causal, diff, -1e30))
            res.append(_dot((cbs[gg_][2] * dec).astype(bf16), xdb))
        ydiag = jnp.where(lo_half, res[0], res[1])
        s_all = sst[:, i * LANES:(i + 1) * LANES, :].reshape(nb * LANES, SSM_STATE)
        s_bf = s_all.astype(bf16)
        full0 = _dot_nt(cbs[g0][1], s_bf)
        full1 = full0 if g1 == g0 else _dot_nt(cbs[g1][1], s_bf)
        yoff = jnp.zeros((r, LANES), f32)
        for b in range(nb):
            blk = jnp.where(lo_half, full0[:, b * LANES:(b + 1) * LANES], full1[:, b * LANES:(b + 1) * LANES])
            yoff = yoff + (jnp.where(rowseq == b, blk, 0.0) if nb > 1 else blk)
        yp = ydiag + yoff * jnp.exp(acp) + dsk_ref[:, sl] * xs
        yp = yp * _silu(z[:, sl])
        ysq = ysq + jnp.sum(yp * yp, axis=-1, keepdims=True)
        ypairs.append(yp)
        xdd = xd * jnp.exp(totp - acp)
        if nb > 1:
            colblk = lax.broadcasted_iota(jnp.int32, (r, nb * LANES), 1) >> 7
            rowblk = lax.broadcasted_iota(jnp.int32, (r, nb * LANES), 0) >> lq
            xblk = jnp.where(colblk == rowblk, jnp.concatenate([xdd] * nb, axis=1), 0.0).astype(bf16)
        else:
            xblk = xdd.astype(bf16)
        upd0 = _dot_tn(xblk, cbs[g0][0])
        if g1 != g0:
            upd1 = _dot_tn(xblk, cbs[g1][0])
            prow = lax.broadcasted_iota(jnp.int32, (nb * LANES, SSM_STATE), 0) & (LANES - 1)
            upd0 = jnp.where(prow < SSM_HEADDIM, upd0, upd1)
        upd0 = upd0.reshape(nb, LANES, SSM_STATE)
        for b in range(nb):
            for hh, half in ((h0, 0), (h1, 1)):
                fac = jnp.exp(jnp.broadcast_to(tot[b * q:b * q + 1, hh:hh + 1], (SSM_HEADDIM, SSM_STATE)))
                rs = slice(i * LANES + half * SSM_HEADDIM, i * LANES + (half + 1) * SSM_HEADDIM)
                sst[b, rs, :] = fac * sst[b, rs, :] + upd0[b, half * SSM_HEADDIM:(half + 1) * SSM_HEADDIM, :]
    rms = lax.rsqrt(ysq * (1.0 / D_WIDTH) + LN_EPS)
    for i in range(SSM_HEADS // 2):
        sl = slice(i * LANES, (i + 1) * LANES)
        yd = ypairs[i] * rms * ng_ref[:, sl]
        o = A_WIDTH + B_WIDTH + C_WIDTH + i * LANES
        y_ref[:, o:o + LANES] = yd.astype(bf16)
    nm_ref[...] = sst[...]


MIXER_WEIGHTS = ("wa", "sgu_w", "sgu_bias_rows", "sgu_ln_g", "sgu_ln_b", "conv_w", "conv_bias", "conv_ln_g",
                 "conv_ln_b", "pool_w", "pool_scale", "ssm_conv_w", "ssm_conv_b", "dt_bias", "a_log", "d_skip",
                 "norm_g")
STATE_SHAPES = ((CONV_W - 1, B_WIDTH), (POOL_BUF, C_WIDTH), (SSM_CONV - 1, XBC_WIDTH), (D_WIDTH, SSM_STATE))


def _mixer_call(x, ada, states, vn_buf, w, l, *, bb, lb, start_pos, mod_off):
    b, seq, _ = x.shape
    r = bb * lb
    nl = seq // lb
    emit_vn = vn_buf is not None
    kern = functools.partial(_mixer_kernel, bb=bb, lb=lb, start_pos=start_pos, emit_vn=emit_vn)

    def state_spec(shape):
        return pl.BlockSpec((None, bb) + shape, lambda i, j: (l, i) + (0,) * len(shape))

    weights = [w[n] for n in MIXER_WEIGHTS]
    extra_in = [vn_buf] if emit_vn else []
    in_specs = ([pl.BlockSpec((bb, lb, D_MODEL), lambda i, j: (i, j, 0)),
                 pl.BlockSpec((None, bb, 6, D_MODEL), lambda i, j: (l, mod_off // bb + i, 0, 0))]
                + [state_spec(s) for s in STATE_SHAPES]
                + [pl.BlockSpec(memory_space=pl.ANY) for _ in extra_in]
                + [_layer_spec(a, l) for a in weights])
    out_shape = [jax.ShapeDtypeStruct((b * seq, Y_WIDTH), bf16)]
    out_shape += [jax.ShapeDtypeStruct(s.shape, f32) for s in states]
    out_specs = [pl.BlockSpec((r, Y_WIDTH), lambda i, j: (i * nl + j, 0))] + [state_spec(s) for s in STATE_SHAPES]
    aliases = {2 + k: 1 + k for k in range(len(STATE_SHAPES))}
    if emit_vn:
        out_shape.append(jax.ShapeDtypeStruct(vn_buf.shape, f32))
        out_specs.append(pl.BlockSpec((None, bb, lb, A_WIDTH), lambda i, j: (l, i, j, 0)))
        aliases[2 + len(STATE_SHAPES)] = 1 + len(STATE_SHAPES)
    scratch = [
        pltpu.VMEM((bb, HIST_CONV + lb, B_WIDTH), f32),
        pltpu.VMEM((SUBLANES - 1, bb, HIST_CONV + lb, B_WIDTH), f32),
        pltpu.VMEM((bb, HIST_POOL + lb, C_WIDTH), f32),
        pltpu.VMEM((bb, HIST_SCONV + lb, XBC_WIDTH), f32),
        pltpu.VMEM((bb, D_WIDTH, SSM_STATE), f32),
        pltpu.VMEM((bb, lb, B_WIDTH), f32),
    ]
    return pl.pallas_call(
        kern, grid=(b // bb, nl), in_specs=in_specs, out_specs=out_specs, out_shape=out_shape,
        scratch_shapes=scratch, input_output_aliases=aliases, compiler_params=_params(),
    )(x, ada, *states, *extra_in, *weights)


def _pack_bf16_pair(a, b):
    ua = lax.bitcast_convert_type(a.astype(bf16).astype(f32), jnp.uint32)
    ub = lax.bitcast_convert_type(b.astype(bf16).astype(f32), jnp.uint32)
    return (ua & jnp.uint32(0xFFFF0000)) | (ub >> 16)


def _unpack_bf16_pair(u):
    hi = lax.bitcast_convert_type(u & jnp.uint32(0xFFFF0000), f32)
    lo = lax.bitcast_convert_type(u << 16, f32)
    return hi, lo


def _merge_kernel(x_ref, mod_ref, y_ref, wg_ref, wbr_ref, wo_ref, lg_ref, lb_ref, wr_ref,
                  x1_ref, xpk_ref, rt_ref, *, bb, lb):
    r = bb * lb
    x = x_ref[...]
    xm = (x * (1.0 + mod_ref[:, 1:2, :]) + mod_ref[:, 0:1, :]).reshape(r, D_MODEL).astype(bf16)
    y = y_ref[...]
    offs = (0, A_WIDTH, A_WIDTH + B_WIDTH, A_WIDTH + B_WIDTH + C_WIDTH, Y_WIDTH)
    merged = jnp.zeros((r, D_MODEL), f32)
    for k in range(N_BRANCH):
        gate = _sigmoid(_dot(xm, wg_ref[:, k * D_MODEL:(k + 1) * D_MODEL]))
        merged = merged + gate * _dot(y[:, offs[k]:offs[k + 1]], wbr_ref[offs[k]:offs[k + 1], :])
    mix = _dot(merged.astype(bf16), wo_ref[...]).reshape(bb, lb, D_MODEL)
    x1 = _layer_norm(DN_ALPHA * x + mod_ref[:, 2:3, :] * mix, lg_ref[...], lb_ref[...])
    x1_ref[...] = x1
    xm2 = (x1 * (1.0 + mod_ref[:, 4:5, :]) + mod_ref[:, 3:4, :]).reshape(r, D_MODEL)
    xpk_ref[...] = _pack_bf16_pair(xm2[:, :HALF], xm2[:, HALF:])

    a1, a2, _ = _split3(xm2)
    lg = _dot(a1, wr_ref[0]) + (_dot(a2, wr_ref[0]) + _dot(a1, wr_ref[1]))
    lane = lax.broadcasted_iota(jnp.int32, (r, LANES), 1).astype(f32)
    neg, big = -1e30, 1e6
    mask_g = lane < 4.0
    lgg = jnp.where(mask_g, lg, neg)
    mg = jnp.max(lgg, axis=-1, keepdims=True)
    grp = jnp.min(jnp.where(lgg == mg, lane, big), axis=-1, keepdims=True)
    p_grp = 1.0 / jnp.sum(jnp.where(mask_g, jnp.exp(lgg - mg), 0.0), axis=-1, keepdims=True)
    lo = 4.0 + grp * E_PER_GROUP
    mask_e = (lane >= lo) & (lane < lo + E_PER_GROUP)
    lge = jnp.where(mask_e, lg, neg)
    v1 = jnp.max(lge, axis=-1, keepdims=True)
    i1 = jnp.min(jnp.where(lge == v1, lane, big), axis=-1, keepdims=True)
    lge2 = jnp.where(lane == i1, neg, lge)
    v2 = jnp.max(lge2, axis=-1, keepdims=True)
    i2 = jnp.min(jnp.where(lge2 == v2, lane, big), axis=-1, keepdims=True)
    e21 = jnp.exp(v2 - v1)
    w1 = p_grp / (1.0 + e21)
    w2 = p_grp * e21 / (1.0 + e21)
    rt = jnp.where(lane == 0.0, i1 - 4.0,
                   jnp.where(lane == 1.0, i2 - 4.0,
                             jnp.where(lane == 2.0, w1, jnp.where(lane == 3.0, w2, 0.0))))
    rt_ref[...] = rt


MERGE_WEIGHTS = ("w_gl", "w_br", "w_o", "ln1_g", "ln1_b", "w_router")


def _merge_call(x, ada, ycat, w, l, *, bb, lb, mod_off):
    b, seq, _ = x.shape
    r = bb * lb
    nl = seq // lb
    n = b * seq
    weights = [w[k] for k in MERGE_WEIGHTS]
    return pl.pallas_call(
        functools.partial(_merge_kernel, bb=bb, lb=lb), grid=(b // bb, nl),
        in_specs=[pl.BlockSpec((bb, lb, D_MODEL), lambda i, j: (i, j, 0)),
                  pl.BlockSpec((None, bb, 6, D_MODEL), lambda i, j: (l, mod_off // bb + i, 0, 0)),
                  pl.BlockSpec((r, Y_WIDTH), lambda i, j: (i * nl + j, 0))] + [_layer_spec(a, l) for a in weights],
        out_specs=(pl.BlockSpec((bb, lb, D_MODEL), lambda i, j: (i, j, 0)),
                   pl.BlockSpec((r, HALF), lambda i, j: (i * nl + j, 0)),
                   pl.BlockSpec((r, LANES), lambda i, j: (i * nl + j, 0))),
        out_shape=(jax.ShapeDtypeStruct((b, seq, D_MODEL), f32),
                   jax.ShapeDtypeStruct((n, HALF), jnp.uint32),
                   jax.ShapeDtypeStruct((n, LANES), f32)),
        compiler_params=_params(),
    )(x, ada, ycat, *weights)


def _rank_kernel(rt_ref, dest_ref, cnt_ref, carry, start, *, t, n_tiles):
    p = pl.program_id(1)
    j = pl.program_id(2)
    rt = rt_ref[...]
    lane = lax.broadcasted_iota(jnp.int32, (t, LANES), 1).astype(f32)
    oh0 = jnp.where(lane == rt[:, 0:1], 1.0, 0.0)
    oh1 = jnp.where(lane == rt[:, 1:2], 1.0, 0.0)
    both = oh0 + oh1
    colsum = jnp.sum(both, axis=0, keepdims=True)

    @pl.when((p == 0) & (j == 0))
    def _():
        carry[...] = jnp.zeros(carry.shape, f32)

    @pl.when(p == 0)
    def _():
        carry[0:1, :] = carry[0:1, :] + colsum

    @pl.when((p == 0) & (j == n_tiles - 1))
    def _():
        c = jnp.broadcast_to(carry[0:1, :], (SUBLANES, LANES))
        cnt_ref[0] = c
        nblk = jnp.floor((c + (MOE_BLOCK - 1.0)) * (1.0 / MOE_BLOCK))
        ri = lax.broadcasted_iota(jnp.int32, (LANES, LANES), 0)
        ci = lax.broadcasted_iota(jnp.int32, (LANES, LANES), 1)
        tri = jnp.where(ri < ci, 1.0, 0.0).astype(bf16)
        start[...] = _dot(nblk.astype(bf16), tri) * float(MOE_BLOCK)
        carry[...] = jnp.zeros(carry.shape, f32)

    @pl.when(p == 1)
    def _():
        rr = lax.broadcasted_iota(jnp.int32, (t, t), 0)
        cc = lax.broadcasted_iota(jnp.int32, (t, t), 1)
        ltri = jnp.where(cc < rr, 1.0, 0.0).astype(bf16)
        base = start[0:1, :] + carry[0:1, :] + _dot(ltri, both.astype(bf16))
        d0 = jnp.sum(oh0 * base, axis=-1, keepdims=True)
        d1 = jnp.sum(oh1 * base, axis=-1, keepdims=True)
        arr = jnp.where(lane == 0.0, d0, jnp.where(lane == 1.0, d1, 0.0))
        dest_ref[0] = arr.T[0:2, :].astype(jnp.int32)
        carry[0:1, :] = carry[0:1, :] + colsum


def _rank_call(route, *, n_groups, gsz):
    t = min(RANK_TILE, gsz)
    n_tiles = gsz // t
    return pl.pallas_call(
        functools.partial(_rank_kernel, t=t, n_tiles=n_tiles),
        grid=(n_groups, 2, n_tiles),
        in_specs=[pl.BlockSpec((t, LANES), lambda g, p, j: (g * n_tiles + j, 0))],
        out_specs=(pl.BlockSpec((1, 2, t), lambda g, p, j: (g, 0, j * p)),
                   pl.BlockSpec((1, SUBLANES, LANES), lambda g, p, j: (g, 0, 0))),
        out_shape=(jax.ShapeDtypeStruct((n_groups, 2, gsz), jnp.int32),
                   jax.ShapeDtypeStruct((n_groups, SUBLANES, LANES), f32)),
        scratch_shapes=[pltpu.VMEM((SUBLANES, LANES), f32), pltpu.VMEM((SUBLANES, LANES), f32)],
        compiler_params=_params(3),
    )(route)


def _expert_kernel(be_ref, nv_ref, dest_ref, x_ref, wg_ref, wu_ref, wd_ref, o_ref, xg, yg, tbl,
                   *, gsz, max_blocks):
    g = pl.program_id(0)
    b = pl.program_id(1)
    pad_row = 2 * gsz

    @pl.when(b == 0)
    def _():
        def init(i, c):
            tbl[i] = pad_row
            return c

        lax.fori_loop(0, max_blocks * MOE_BLOCK, init, 0, unroll=8)

        def fill(t, c):
            tbl[dest_ref[0, 0, t]] = t
            tbl[dest_ref[0, 1, t]] = gsz + t
            return c

        lax.fori_loop(0, gsz, fill, 0, unroll=8)
        o_ref[0, pad_row:pad_row + SUBLANES, :] = jnp.zeros((SUBLANES, HALF), jnp.uint32)

    @pl.when(b < nv_ref[g])
    def _():
        def gather(i, c):
            s = tbl[b * MOE_BLOCK + i]
            xg[pl.ds(i, 1), :] = x_ref[pl.ds(s & (gsz - 1), 1), :]
            return c

        lax.fori_loop(0, MOE_BLOCK, gather, 0, unroll=8)
        hi, lo = _unpack_bf16_pair(xg[...])
        hi = hi.astype(bf16)
        lo = lo.astype(bf16)
        hg = _dot(hi, wg_ref[:HALF, :]) + _dot(lo, wg_ref[HALF:, :])
        hu = _dot(hi, wu_ref[:HALF, :]) + _dot(lo, wu_ref[HALF:, :])
        hid = (_silu(hg) * hu).astype(bf16)
        y = _dot(hid, wd_ref[...])
        yg[...] = _pack_bf16_pair(y[:, :HALF], y[:, HALF:])

        def scatter(i, c):
            s = tbl[b * MOE_BLOCK + i]
            o_ref[0, pl.ds(s, 1), :] = yg[pl.ds(i, 1), :]
            return c

        lax.fori_loop(0, MOE_BLOCK, scatter, 0, unroll=8)


def _expert_call(xpk, block_e, n_valid, dest, w, l, *, n_groups, gsz, max_blocks):
    assert gsz & (gsz - 1) == 0

    def w_spec(shape):
        return pl.BlockSpec((None, None) + shape, lambda g, b, be, nv: (l, be[g * max_blocks + b], 0, 0))

    single = pl.Buffered(1)
    grid_spec = pltpu.PrefetchScalarGridSpec(
        num_scalar_prefetch=2,
        grid=(n_groups, max_blocks),
        in_specs=[
            pl.BlockSpec((1, 2, gsz), lambda g, b, be, nv: (g, 0, 0), memory_space=pltpu.SMEM),
            pl.BlockSpec((gsz, HALF), lambda g, b, be, nv: (g, 0), pipeline_mode=single),
            w_spec((D_MODEL, EXPERT_FF)), w_spec((D_MODEL, EXPERT_FF)), w_spec((EXPERT_FF, D_MODEL)),
        ],
        out_specs=pl.BlockSpec((1, 2 * gsz + SUBLANES, HALF), lambda g, b, be, nv: (g, 0, 0),
                               pipeline_mode=single),
        scratch_shapes=[pltpu.VMEM((MOE_BLOCK, HALF), jnp.uint32), pltpu.VMEM((MOE_BLOCK, HALF), jnp.uint32),
                        pltpu.SMEM((max_blocks * MOE_BLOCK,), jnp.int32)],
    )
    return pl.pallas_call(
        functools.partial(_expert_kernel, gsz=gsz, max_blocks=max_blocks), grid_spec=grid_spec,
        out_shape=jax.ShapeDtypeStruct((n_groups, 2 * gsz + SUBLANES, HALF), jnp.uint32),
        compiler_params=_params(),
    )(block_e, n_valid, dest, xpk, w["w_e_gate"], w["w_e_up"], w["w_e_down"])


def _combine_kernel(x_ref, mod_ref, y0_ref, y1_ref, rt_ref, lg_ref, lb_ref, o_ref, *, bb, lb):
    rt = rt_ref[...]
    w0 = rt[:, 2:3]
    w1 = rt[:, 3:4]
    h0, l0 = _unpack_bf16_pair(y0_ref[0])
    h1, l1 = _unpack_bf16_pair(y1_ref[0])
    f = jnp.concatenate([w0 * h0 + w1 * h1, w0 * l0 + w1 * l1], axis=1).reshape(bb, lb, D_MODEL)
    t = DN_ALPHA * x_ref[...] + mod_ref[:, 5:6, :] * f
    o_ref[...] = _layer_norm(t, lg_ref[...], lb_ref[...])


def _combine_call(x1, ada, ys, route, w, l, *, bb, lb, gsz, mod_off):
    b, seq, _ = x1.shape
    r = bb * lb
    nl = seq // lb
    tiles_per_group = gsz // r

    def y_map(k):
        def m(i, j):
            t = i * nl + j
            return (t // tiles_per_group, k * tiles_per_group + t % tiles_per_group, 0)
        return m

    return pl.pallas_call(
        functools.partial(_combine_kernel, bb=bb, lb=lb),
        grid=(b // bb, nl),
        in_specs=[pl.BlockSpec((bb, lb, D_MODEL), lambda i, j: (i, j, 0)),
                  pl.BlockSpec((None, bb, 6, D_MODEL), lambda i, j: (l, mod_off // bb + i, 0, 0)),
                  pl.BlockSpec((1, r, HALF), y_map(0)),
                  pl.BlockSpec((1, r, HALF), y_map(1)),
                  pl.BlockSpec((r, LANES), lambda i, j: (i * nl + j, 0)),
                  _layer_spec(w["ln2_g"], l), _layer_spec(w["ln2_b"], l)],
        out_specs=pl.BlockSpec((bb, lb, D_MODEL), lambda i, j: (i, j, 0)),
        out_shape=jax.ShapeDtypeStruct((b, seq, D_MODEL), f32),
        compiler_params=_params(),
    )(x1, ada, ys, ys, route, w["ln2_g"], w["ln2_b"])


def _moe(x1, ada, xpk, route, w, l, *, bb, lb, gsz, mod_off):
    n = xpk.shape[0]
    n_groups = n // gsz
    max_blocks = (2 * gsz + N_EXPERTS * (MOE_BLOCK - 1)) // MOE_BLOCK
    dest, counts = _rank_call(route, n_groups=n_groups, gsz=gsz)
    cnt = counts[:, 0, :N_EXPERTS].astype(jnp.int32)
    end_blk = jnp.cumsum((cnt + MOE_BLOCK - 1) // MOE_BLOCK, axis=-1)
    n_valid = end_blk[:, -1]
    blk = jnp.arange(max_blocks, dtype=jnp.int32)
    block_e = jnp.sum(blk[None, :, None] >= end_blk[:, None, :], axis=-1).astype(jnp.int32)
    block_e = jnp.minimum(block_e, N_EXPERTS - 1).reshape(n_groups * max_blocks)
    ys = _expert_call(xpk, block_e, n_valid, dest, w, l, n_groups=n_groups, gsz=gsz, max_blocks=max_blocks)
    return _combine_call(x1, ada, ys, route, w, l, bb=bb, lb=lb, gsz=gsz, mod_off=mod_off)


def _win_kernel(w_ref, wa_ref, wgl_ref):
    wa_ref[0] = w_ref[0, :, 0:WA_WIDTH].astype(bf16)
    wgl_ref[0] = w_ref[0, :, GATE_OFF:GATE_OFF + N_BRANCH * D_MODEL].astype(bf16)


def _win_call(w_in):
    rows = 128
    width = w_in.shape[-1]
    return pl.pallas_call(
        _win_kernel, grid=(DEPTH, D_MODEL // rows),
        in_specs=[pl.BlockSpec((1, rows, width), lambda l, i: (l, i, 0))],
        out_specs=(pl.BlockSpec((1, rows, WA_WIDTH), lambda l, i: (l, i, 0)),
                   pl.BlockSpec((1, rows, N_BRANCH * D_MODEL), lambda l, i: (l, i, 0))),
        out_shape=(jax.ShapeDtypeStruct((DEPTH, D_MODEL, WA_WIDTH), bf16),
                   jax.ShapeDtypeStruct((DEPTH, D_MODEL, N_BRANCH * D_MODEL), bf16)),
        compiler_params=_params(),
    )(w_in)


def _prep_weights(p):
    wa, w_gl = _win_call(p["w_in"])
    lane_pad = lambda v: jnp.pad(v, ((0, 0), (0, LANES - v.shape[1])))[:, None, :]
    row = lambda v: v[:, None, :]
    wr = jnp.pad(jnp.concatenate([p["router_g"], p["router_e"]], axis=-1),
                 ((0, 0), (0, 0), (0, LANES - 4 - N_EXPERTS)))
    wr_hi = wr.astype(bf16)
    wr_lo = (wr - wr_hi.astype(f32)).astype(bf16)
    return dict(
        wa=wa, w_gl=w_gl,
        sgu_ln_g=row(p["sgu_ln_g"]), sgu_ln_b=row(p["sgu_ln_b"]),
        conv_w=p["conv_w"], conv_bias=row(p["conv_bias"]),
        conv_ln_g=row(p["conv_ln_g"]), conv_ln_b=row(p["conv_ln_b"]),
        pool_w=p["pool_w"], pool_scale=row(p["pool_scale"]),
        ssm_conv_w=p["ssm_conv_w"], ssm_conv_b=row(p["ssm_conv_b"]),
        dt_bias=lane_pad(p["ssm_dt_bias"]), a_log=lane_pad(p["ssm_a_log"]),
        d_skip=row(jnp.repeat(p["ssm_d"], SSM_HEADDIM, axis=-1)), norm_g=row(p["ssm_norm_g"]),
        w_br=jnp.concatenate([p["w_br_a"], p["w_br_b"], p["w_br_c"], p["w_br_d"]], axis=1).astype(bf16),
        w_o=p["w_o"].astype(bf16),
        ln1_g=row(p["ln1_g"]), ln1_b=row(p["ln1_b"]),
        w_router=jnp.stack([wr_hi, wr_lo], axis=1),
        w_e_gate=p["w_e_gate"].astype(bf16), w_e_up=p["w_e_up"].astype(bf16), w_e_down=p["w_e_down"].astype(bf16),
        ln2_g=row(p["ln2_g"]), ln2_b=row(p["ln2_b"]),
    )


def _prep_sgu(p, q, r):
    nb = r // q
    sgu_w = jnp.tile(p["sgu_w"][:, :, :q, :q], (1, 1, nb, nb))
    bias = jnp.tile(p["sgu_b"][:, :, :q], (1, 1, nb))
    bias = jnp.repeat(jnp.swapaxes(bias, 1, 2), LANES, axis=-1)
    return dict(sgu_w=sgu_w, sgu_bias_rows=bias)


PROMPT_MIX = (1, 128)
SAMPLE_MIX = (8, 8)
PROMPT_TOK = (1, 256)
SAMPLE_TOK = (32, 8)
PROMPT_GROUP = 4096
SAMPLE_GROUP = 1024


def kernel(x_prompt, x_sample, state_conv, state_pool, state_ssm_conv, state_ssm, c_prompt, c_sample,
           w_ada, b_ada, w_in, sgu_ln_g, sgu_ln_b, sgu_w, sgu_b, conv_w, conv_bias, conv_ln_g, conv_ln_b,
           pool_w, pool_scale, ssm_conv_w, ssm_conv_b, ssm_dt_bias, ssm_a_log, ssm_d, ssm_norm_g,
           w_br_a, w_br_b, w_br_c, w_br_d, w_o, ln1_g, ln1_b, router_g, router_e, w_e_gate, w_e_up,
           w_e_down, ln2_g, ln2_b):
    return _forward(x_prompt, x_sample, state_conv, state_pool, state_ssm_conv, state_ssm, c_prompt, c_sample,
                    w_ada, b_ada, w_in, sgu_ln_g, sgu_ln_b, sgu_w, sgu_b, conv_w, conv_bias, conv_ln_g, conv_ln_b,
                    pool_w, pool_scale, ssm_conv_w, ssm_conv_b, ssm_dt_bias, ssm_a_log, ssm_d, ssm_norm_g,
                    w_br_a, w_br_b, w_br_c, w_br_d, w_o, ln1_g, ln1_b, router_g, router_e, w_e_gate, w_e_up,
                    w_e_down, ln2_g, ln2_b)


def _forward(x_prompt, x_sample, state_conv, state_pool, state_ssm_conv, state_ssm, c_prompt, c_sample,
             w_ada, b_ada, w_in, sgu_ln_g, sgu_ln_b, sgu_w, sgu_b, conv_w, conv_bias, conv_ln_g, conv_ln_b,
             pool_w, pool_scale, ssm_conv_w, ssm_conv_b, ssm_dt_bias, ssm_a_log, ssm_d, ssm_norm_g,
             w_br_a, w_br_b, w_br_c, w_br_d, w_o, ln1_g, ln1_b, router_g, router_e, w_e_gate, w_e_up,
             w_e_down, ln2_g, ln2_b, prompt_group=PROMPT_GROUP, sample_group=SAMPLE_GROUP,
             sample_tok=SAMPLE_TOK):
    p = dict(w_in=w_in, sgu_ln_g=sgu_ln_g, sgu_ln_b=sgu_ln_b, sgu_w=sgu_w, sgu_b=sgu_b, conv_w=conv_w,
             conv_bias=conv_bias, conv_ln_g=conv_ln_g, conv_ln_b=conv_ln_b, pool_w=pool_w, pool_scale=pool_scale,
             ssm_conv_w=ssm_conv_w, ssm_conv_b=ssm_conv_b, ssm_dt_bias=ssm_dt_bias, ssm_a_log=ssm_a_log,
             ssm_d=ssm_d, ssm_norm_g=ssm_norm_g, w_br_a=w_br_a, w_br_b=w_br_b, w_br_c=w_br_c, w_br_d=w_br_d,
             w_o=w_o, ln1_g=ln1_g, ln1_b=ln1_b, router_g=router_g, router_e=router_e, ln2_g=ln2_g, ln2_b=ln2_b,
             w_e_gate=w_e_gate, w_e_up=w_e_up, w_e_down=w_e_down)
    bp = x_prompt.shape[0]
    bs = x_sample.shape[0]
    w = _prep_weights(p)
    ada = _ada_call(jnp.concatenate([c_sample, c_prompt], axis=0), w_ada, b_ada)
    ada = ada.reshape(DEPTH, bs + bp, 6, D_MODEL)

    groups = [
        dict(x=x_prompt, mix=PROMPT_MIX, tok=PROMPT_TOK, gsz=prompt_group, start=0, mod_off=bs, vn=None,
             states=tuple(jnp.zeros((DEPTH, bp) + s, f32) for s in STATE_SHAPES)),
        dict(x=x_sample, mix=SAMPLE_MIX, tok=sample_tok, gsz=sample_group, start=PAST_LEN, mod_off=0,
             vn=jnp.zeros((DEPTH, bs, x_sample.shape[1], A_WIDTH), f32),
             states=(state_conv, state_pool, state_ssm_conv, state_ssm.reshape(DEPTH, bs, D_WIDTH, SSM_STATE))),
    ]
    for g in groups:
        g["w"] = dict(w, **_prep_sgu(p, g["mix"][1], g["mix"][0] * g["mix"][1]))
    for l in range(DEPTH):
        for g in groups:
            bb, lb = g["mix"]
            res = _mixer_call(g["x"], ada, g["states"], g["vn"], g["w"], l, bb=bb, lb=lb, start_pos=g["start"],
                              mod_off=g["mod_off"])
            g["states"] = tuple(res[1:5])
            if g["vn"] is not None:
                g["vn"] = res[5]
            bb, lb = g["tok"]
            x1, xpk, route = _merge_call(g["x"], ada, res[0], g["w"], l, bb=bb, lb=lb, mod_off=g["mod_off"])
            g["x"] = _moe(x1, ada, xpk, route, g["w"], l, bb=bb, lb=lb, gsz=g["gsz"], mod_off=g["mod_off"])

    def states_out(g, b):
        c, pl_, sc, sm = g["states"]
        return c, pl_, sc, sm.reshape(DEPTH, b, SSM_HEADS, SSM_HEADDIM, SSM_STATE)

    return ((groups[0]["x"], groups[1]["x"]) + states_out(groups[0], bp) + states_out(groups[1], bs)
            + (groups[1]["vn"],))
```

```python
import functools
import math

import jax
import jax.numpy as jnp
import numpy as np
from jax import lax
from jax.experimental import pallas as pl
from jax.experimental.pallas import tpu as pltpu

D_MODEL = 1024
DEPTH = 4
A_WIDTH = 512
A_GROUPS = 4
B_WIDTH = 512
CONV_W = 31
C_WIDTH = 512
POOL_WINDOWS = (2, 4, 8, 16)
POOL_BUF = 15
SSM_HEADS = 12
SSM_HEADDIM = 64
D_WIDTH = SSM_HEADS * SSM_HEADDIM
SSM_GROUPS = 4
SSM_STATE = 128
SSM_CONV = 4
XBC_WIDTH = D_WIDTH + 2 * SSM_GROUPS * SSM_STATE
N_BRANCH = 4
E_PER_GROUP = 8
N_EXPERTS = 32
EXPERT_FF = 512
MOE_BLOCK = 128
DN_ALPHA = (2 * DEPTH) ** 0.25
LN_EPS = 1e-5
PAST_LEN = 16384
Y_WIDTH = A_WIDTH + B_WIDTH + C_WIDTH + D_WIDTH
HALF = D_MODEL // 2

OFF_U, OFF_V, OFF_GA, OFF_GG, OFF_P, OFF_Z, OFF_XBC, OFF_DT = 0, 512, 1024, 1536, 2048, 2560, 3328, 5120
GATE_OFF = OFF_DT + SSM_HEADS
WA_WIDTH = 5248
LANES = 128
SUBLANES = 8
HIST_CONV = 32
HIST_POOL = 16
HIST_SCONV = 8
VMEM_LIMIT = 56 * 1024 * 1024
RANK_TILE = 512

f32 = jnp.float32
bf16 = jnp.bfloat16


def _dot(a, b):
    return jnp.dot(a, b, preferred_element_type=f32)


def _dot_nt(a, b):
    return lax.dot_general(a, b, (((1,), (1,)), ((), ())), preferred_element_type=f32)


def _dot_tn(a, b):
    return lax.dot_general(a, b, (((0,), (0,)), ((), ())), preferred_element_type=f32)


def _split3(x):
    h1 = x.astype(bf16)
    r1 = x - h1.astype(f32)
    h2 = r1.astype(bf16)
    r2 = r1 - h2.astype(f32)
    return h1, h2, r2.astype(bf16)


def _sigmoid(x):
    return 0.5 * (jnp.tanh(0.5 * x) + 1.0)


def _silu(x):
    return x * _sigmoid(x)


def _gelu(x):
    return 0.5 * x * (1.0 + jnp.tanh(math.sqrt(2.0 / math.pi) * (x + 0.044715 * (x * x * x))))


def _softplus(x):
    return jnp.maximum(x, 0.0) + jnp.log1p(jnp.exp(-jnp.abs(x)))


def _layer_norm(x, g, b):
    mu = jnp.mean(x, axis=-1, keepdims=True)
    xc = x - mu
    var = jnp.mean(xc * xc, axis=-1, keepdims=True)
    return xc * lax.rsqrt(var + LN_EPS) * g + b


def _layer_spec(a, l):
    nd = a.ndim
    return pl.BlockSpec((None,) + a.shape[1:], lambda *_: (l,) + (0,) * (nd - 1))


def _params(n_axes=2):
    return pltpu.CompilerParams(dimension_semantics=("arbitrary",) * n_axes, vmem_limit_bytes=VMEM_LIMIT)


def _ada_kernel(c_ref, w_ref, b_ref, o_ref):
    a = _silu(c_ref[...]).astype(bf16)
    o_ref[0] = _dot(a, w_ref[0].astype(bf16)) + b_ref[0]


def _ada_call(c_all, w_ada, b_ada):
    n = c_all.shape[0]
    return pl.pallas_call(
        _ada_kernel,
        grid=(DEPTH, 6),
        in_specs=[
            pl.BlockSpec((n, D_MODEL), lambda l, j: (0, 0)),
            pl.BlockSpec((1, D_MODEL, D_MODEL), lambda l, j: (l, 0, j)),
            pl.BlockSpec((1, 1, D_MODEL), lambda l, j: (l, 0, j)),
        ],
        out_specs=pl.BlockSpec((1, n, D_MODEL), lambda l, j: (l, 0, j)),
        out_shape=jax.ShapeDtypeStruct((DEPTH, n, 6 * D_MODEL), f32),
        compiler_params=_params(),
    )(c_all, w_ada, b_ada.reshape(DEPTH, 1, 6 * D_MODEL))


def _mixer_kernel(x_ref, mod_ref, stc_ref, stp_ref, sts_ref, stm_ref, *rest, bb, lb, start_pos, emit_vn):
    if emit_vn:
        rest = rest[1:]
    (wa_ref, sgw_ref, sgb_ref, slg_ref, slb_ref, cw_ref, cb_ref, clg_ref, clb_ref, pw_ref, ps_ref,
     scw_ref, scb_ref, dtb_ref, alog_ref, dsk_ref, ng_ref, y_ref, nc_ref, np_ref, ns_ref, nm_ref) = rest[:22]
    vn_ref = rest[22] if emit_vn else None
    hbuf, hsh, pbuf, xbuf, sst, cbuf = rest[-6:]
    r = bb * lb
    nb, q = bb, lb
    lq = q.bit_length() - 1
    j = pl.program_id(1)

    @pl.when(j == 0)
    def _():
        hbuf[:, HIST_CONV - (CONV_W - 1):HIST_CONV, :] = stc_ref[...]
        pbuf[:, HIST_POOL - POOL_BUF:HIST_POOL, :] = stp_ref[...]
        xbuf[:, HIST_SCONV - (SSM_CONV - 1):HIST_SCONV, :] = sts_ref[...]
        sst[...] = stm_ref[...]

    x = x_ref[...]
    xm = (x * (1.0 + mod_ref[:, 1:2, :]) + mod_ref[:, 0:1, :]).reshape(r, D_MODEL).astype(bf16)

    row_i = lax.broadcasted_iota(jnp.int32, (r, r), 0)
    col_i = lax.broadcasted_iota(jnp.int32, (r, r), 1)
    same = (row_i >> lq) == (col_i >> lq)
    causal = same & ((col_i & (q - 1)) <= (row_i & (q - 1)))
    causal_t = same & ((row_i & (q - 1)) <= (col_i & (q - 1)))

    u = _gelu(_dot(xm, wa_ref[:, OFF_U:OFF_U + A_WIDTH]))
    v = _gelu(_dot(xm, wa_ref[:, OFF_V:OFF_V + A_WIDTH]))
    vn = _layer_norm(v, slg_ref[...], slb_ref[...])
    if emit_vn:
        vn_ref[...] = vn.reshape(bb, lb, A_WIDTH)
    vnb = vn.astype(bf16)
    for g in range(A_GROUPS):
        sl = slice(g * LANES, (g + 1) * LANES)
        wmix = jnp.where(causal, sgw_ref[g], 0.0).astype(bf16)
        s = _dot(wmix, vnb[:, sl]) + sgb_ref[:, sl]
        y_ref[:, sl] = (u[:, sl] * s).astype(bf16)

    ga = _dot(xm, wa_ref[:, OFF_GA:OFF_GA + B_WIDTH])
    gg = _dot(xm, wa_ref[:, OFF_GG:OFF_GG + B_WIDTH])
    h = ga * _sigmoid(gg)
    hbuf[:, HIST_CONV:HIST_CONV + lb, :] = h.reshape(bb, lb, B_WIDTH)
    base = HIST_CONV - (CONV_W - 1)
    span = HIST_CONV + lb - SUBLANES
    for s in range(1, SUBLANES):
        hsh[s - 1, :, 0:span, :] = hbuf[:, s:s + span, :]
    lc = min(lb, 64)
    for rc in range(lb // lc):
        acc = jnp.zeros((bb, lc, B_WIDTH), f32) + cb_ref[...]
        for k in range(CONV_W):
            o = base + k + rc * lc
            s, a = o % SUBLANES, o - o % SUBLANES
            tap = hbuf[:, a:a + lc, :] if s == 0 else hsh[s - 1, :, a:a + lc, :]
            acc = acc + tap * cw_ref[k:k + 1, :]
        cbuf[:, rc * lc:(rc + 1) * lc, :] = acc
    yb = _silu(_layer_norm(cbuf[...].reshape(r, B_WIDTH), clg_ref[...], clb_ref[...]))
    y_ref[:, A_WIDTH:A_WIDTH + B_WIDTH] = yb.astype(bf16)
    new_c = hbuf[:, lb + base:lb + HIST_CONV, :]
    nc_ref[...] = new_c
    hbuf[:, base:HIST_CONV, :] = new_c

    pin = _dot(xm, wa_ref[:, OFF_P:OFF_P + C_WIDTH])
    pbuf[:, HIST_POOL:HIST_POOL + lb, :] = pin.reshape(bb, lb, C_WIDTH)
    pos = (start_pos + j * lb + lax.broadcasted_iota(jnp.int32, (bb, lb, LANES), 1)).astype(f32)
    for gi, win in enumerate(POOL_WINDOWS):
        sl = slice(gi * LANES, (gi + 1) * LANES)
        acc = pbuf[:, HIST_POOL:HIST_POOL + lb, sl]
        for i in range(1, win):
            acc = acc + pbuf[:, HIST_POOL - i:HIST_POOL - i + lb, sl]
        cnt = jnp.minimum(float(win), pos + 1.0)
        mixed = (acc / cnt - pbuf[:, HIST_POOL:HIST_POOL + lb, sl]).reshape(r, LANES)
        out = _dot(mixed.astype(bf16), pw_ref[gi].astype(bf16)) * ps_ref[:, sl]
        y_ref[:, A_WIDTH + B_WIDTH + gi * LANES:A_WIDTH + B_WIDTH + (gi + 1) * LANES] = out.astype(bf16)
    new_p = pbuf[:, lb + HIST_POOL - POOL_BUF:lb + HIST_POOL, :]
    np_ref[...] = new_p
    pbuf[:, HIST_POOL - POOL_BUF:HIST_POOL, :] = new_p

    z = _dot(xm, wa_ref[:, OFF_Z:OFF_Z + D_WIDTH])
    xbc = _dot(xm, wa_ref[:, OFF_XBC:OFF_XBC + XBC_WIDTH])
    dtr = _dot(xm, wa_ref[:, OFF_DT:OFF_DT + LANES])
    xbuf[:, HIST_SCONV:HIST_SCONV + lb, :] = xbc.reshape(bb, lb, XBC_WIDTH)
    sbase = HIST_SCONV - (SSM_CONV - 1)
    acc = jnp.zeros((bb, lb, XBC_WIDTH), f32) + scb_ref[...]
    for k in range(SSM_CONV):
        acc = acc + xbuf[:, sbase + k:sbase + k + lb, :] * scw_ref[k:k + 1, :]
    xc = _silu(acc).reshape(r, XBC_WIDTH)
    new_s = xbuf[:, lb + sbase:lb + HIST_SCONV, :]
    ns_ref[...] = new_s
    xbuf[:, sbase:HIST_SCONV, :] = new_s

    dt = _softplus(dtr + dtb_ref[...])
    da = dt * (-jnp.exp(alog_ref[...]))
    d1, d2, d3 = _split3(da)
    mc = jnp.where(causal, 1.0, 0.0).astype(bf16)
    mct = jnp.where(causal_t, 1.0, 0.0).astype(bf16)
    ms = jnp.where(same, 1.0, 0.0).astype(bf16)
    acs = _dot(mc, d1) + _dot(mc, d2) + _dot(mc, d3)
    acs_t = _dot_tn(d1, mct) + _dot_tn(d2, mct) + _dot_tn(d3, mct)
    tot = _dot(ms, d1) + _dot(ms, d2) + _dot(ms, d3)

    lane = lax.broadcasted_iota(jnp.int32, (r, LANES), 1)
    lo_half = lane < SSM_HEADDIM
    cbs = []
    for g in range(SSM_GROUPS):
        bg = xc[:, D_WIDTH + g * SSM_STATE:D_WIDTH + (g + 1) * SSM_STATE].astype(bf16)
        cg = xc[:, D_WIDTH + (SSM_GROUPS + g) * SSM_STATE:D_WIDTH + (SSM_GROUPS + g + 1) * SSM_STATE].astype(bf16)
        cbs.append((bg, cg, _dot_nt(cg, bg)))
    rep = SSM_HEADS // SSM_GROUPS
    rowseq = lax.broadcasted_iota(jnp.int32, (r, LANES), 0) >> lq
    ysq = jnp.zeros((r, 1), f32)
    ypairs = []
    for i in range(SSM_HEADS // 2):
        h0, h1 = 2 * i, 2 * i + 1
        g0, g1 = h0 // rep, h1 // rep
        sl = slice(i * LANES, (i + 1) * LANES)
        xs = xc[:, sl]
        dtp = jnp.where(lo_half, dt[:, h0:h0 + 1], dt[:, h1:h1 + 1])
        acp = jnp.where(lo_half, acs[:, h0:h0 + 1], acs[:, h1:h1 + 1])
        totp = jnp.where(lo_half, tot[:, h0:h0 + 1], tot[:, h1:h1 + 1])
        xd = xs * dtp
        xdb = xd.astype(bf16)
        res = []
        for hh, gg_ in ((h0, g0), (h1, g1)):
            diff = acs[:, hh:hh + 1] - acs_t[hh:hh + 1, :]
            dec = jnp.exp(jnp.where(causal, diff, -1e30))
            res.append(_dot((cbs[gg_][2] * dec).astype(bf16), xdb))
        ydiag = jnp.where(lo_half, res[0], res[1])
        s_all = sst[:, i * LANES:(i + 1) * LANES, :].reshape(nb * LANES, SSM_STATE)
        s_bf = s_all.astype(bf16)
        full0 = _dot_nt(cbs[g0][1], s_bf)
        full1 = full0 if g1 == g0 else _dot_nt(cbs[g1][1], s_bf)
        yoff = jnp.zeros((r, LANES), f32)
        for b in range(nb):
            blk = jnp.where(lo_half, full0[:, b * LANES:(b + 1) * LANES], full1[:, b * LANES:(b + 1) * LANES])
            yoff = yoff + (jnp.where(rowseq == b, blk, 0.0) if nb > 1 else blk)
        yp = ydiag + yoff * jnp.exp(acp) + dsk_ref[:, sl] * xs
        yp = yp * _silu(z[:, sl])
        ysq = ysq + jnp.sum(yp * yp, axis=-1, keepdims=True)
        ypairs.append(yp)
        xdd = xd * jnp.exp(totp - acp)
        if nb > 1:
            colblk = lax.broadcasted_iota(jnp.int32, (r, nb * LANES), 1) >> 7
            rowblk = lax.broadcasted_iota(jnp.int32, (r, nb * LANES), 0) >> lq
            xblk = jnp.where(colblk == rowblk, jnp.concatenate([xdd] * nb, axis=1), 0.0).astype(bf16)
        else:
            xblk = xdd.astype(bf16)
        upd0 = _dot_tn(xblk, cbs[g0][0])
        if g1 != g0:
            upd1 = _dot_tn(xblk, cbs[g1][0])
            prow = lax.broadcasted_iota(jnp.int32, (nb * LANES, SSM_STATE), 0) & (LANES - 1)
            upd0 = jnp.where(prow < SSM_HEADDIM, upd0, upd1)
        upd0 = upd0.reshape(nb, LANES, SSM_STATE)
        for b in range(nb):
            for hh, half in ((h0, 0), (h1, 1)):
                fac = jnp.exp(jnp.broadcast_to(tot[b * q:b * q + 1, hh:hh + 1], (SSM_HEADDIM, SSM_STATE)))
                rs = slice(i * LANES + half * SSM_HEADDIM, i * LANES + (half + 1) * SSM_HEADDIM)
                sst[b, rs, :] = fac * sst[b, rs, :] + upd0[b, half * SSM_HEADDIM:(half + 1) * SSM_HEADDIM, :]
    rms = lax.rsqrt(ysq * (1.0 / D_WIDTH) + LN_EPS)
    for i in range(SSM_HEADS // 2):
        sl = slice(i * LANES, (i + 1) * LANES)
        yd = ypairs[i] * rms * ng_ref[:, sl]
        o = A_WIDTH + B_WIDTH + C_WIDTH + i * LANES
        y_ref[:, o:o + LANES] = yd.astype(bf16)
    nm_ref[...] = sst[...]


MIXER_WEIGHTS = ("wa", "sgu_w", "sgu_bias_rows", "sgu_ln_g", "sgu_ln_b", "conv_w", "conv_bias", "conv_ln_g",
                 "conv_ln_b", "pool_w", "pool_scale", "ssm_conv_w", "ssm_conv_b", "dt_bias", "a_log", "d_skip",
                 "norm_g")
STATE_SHAPES = ((CONV_W - 1, B_WIDTH), (POOL_BUF, C_WIDTH), (SSM_CONV - 1, XBC_WIDTH), (D_WIDTH, SSM_STATE))


def _mixer_call(x, ada, states, vn_buf, w, l, *, bb, lb, start_pos, mod_off):
    b, seq, _ = x.shape
    r = bb * lb
    nl = seq // lb
    emit_vn = vn_buf is not None
    kern = functools.partial(_mixer_kernel, bb=bb, lb=lb, start_pos=start_pos, emit_vn=emit_vn)

    def state_spec(shape):
        return pl.BlockSpec((None, bb) + shape, lambda i, j: (l, i) + (0,) * len(shape))

    weights = [w[n] for n in MIXER_WEIGHTS]
    extra_in = [vn_buf] if emit_vn else []
    in_specs = ([pl.BlockSpec((bb, lb, D_MODEL), lambda i, j: (i, j, 0)),
                 pl.BlockSpec((None, bb, 6, D_MODEL), lambda i, j: (l, mod_off // bb + i, 0, 0))]
                + [state_spec(s) for s in STATE_SHAPES]
                + [pl.BlockSpec(memory_space=pl.ANY) for _ in extra_in]
                + [_layer_spec(a, l) for a in weights])
    out_shape = [jax.ShapeDtypeStruct((b * seq, Y_WIDTH), bf16)]
    out_shape += [jax.ShapeDtypeStruct(s.shape, f32) for s in states]
    out_specs = [pl.BlockSpec((r, Y_WIDTH), lambda i, j: (i * nl + j, 0))] + [state_spec(s) for s in STATE_SHAPES]
    aliases = {2 + k: 1 + k for k in range(len(STATE_SHAPES))}
    if emit_vn:
        out_shape.append(jax.ShapeDtypeStruct(vn_buf.shape, f32))
        out_specs.append(pl.BlockSpec((None, bb, lb, A_WIDTH), lambda i, j: (l, i, j, 0)))
        aliases[2 + len(STATE_SHAPES)] = 1 + len(STATE_SHAPES)
    scratch = [
        pltpu.VMEM((bb, HIST_CONV + lb, B_WIDTH), f32),
        pltpu.VMEM((SUBLANES - 1, bb, HIST_CONV + lb, B_WIDTH), f32),
        pltpu.VMEM((bb, HIST_POOL + lb, C_WIDTH), f32),
        pltpu.VMEM((bb, HIST_SCONV + lb, XBC_WIDTH), f32),
        pltpu.VMEM((bb, D_WIDTH, SSM_STATE), f32),
        pltpu.VMEM((bb, lb, B_WIDTH), f32),
    ]
    return pl.pallas_call(
        kern, grid=(b // bb, nl), in_specs=in_specs, out_specs=out_specs, out_shape=out_shape,
        scratch_shapes=scratch, input_output_aliases=aliases, compiler_params=_params(),
    )(x, ada, *states, *extra_in, *weights)


def _pack_bf16_pair(a, b):
    ua = lax.bitcast_convert_type(a.astype(bf16).astype(f32), jnp.uint32)
    ub = lax.bitcast_convert_type(b.astype(bf16).astype(f32), jnp.uint32)
    return (ua & jnp.uint32(0xFFFF0000)) | (ub >> 16)


def _unpack_bf16_pair(u):
    hi = lax.bitcast_convert_type(u & jnp.uint32(0xFFFF0000), f32)
    lo = lax.bitcast_convert_type(u << 16, f32)
    return hi, lo


def _merge_kernel(x_ref, mod_ref, y_ref, wg_ref, wbr_ref, wo_ref, lg_ref, lb_ref, wr_ref,
                  x1_ref, xpk_ref, rt_ref, *, bb, lb):
    r = bb * lb
    x = x_ref[...]
    xm = (x * (1.0 + mod_ref[:, 1:2, :]) + mod_ref[:, 0:1, :]).reshape(r, D_MODEL).astype(bf16)
    y = y_ref[...]
    offs = (0, A_WIDTH, A_WIDTH + B_WIDTH, A_WIDTH + B_WIDTH + C_WIDTH, Y_WIDTH)
    merged = jnp.zeros((r, D_MODEL), f32)
    for k in range(N_BRANCH):
        gate = _sigmoid(_dot(xm, wg_ref[:, k * D_MODEL:(k + 1) * D_MODEL]))
        merged = merged + gate * _dot(y[:, offs[k]:offs[k + 1]], wbr_ref[offs[k]:offs[k + 1], :])
    mix = _dot(merged.astype(bf16), wo_ref[...]).reshape(bb, lb, D_MODEL)
    x1 = _layer_norm(DN_ALPHA * x + mod_ref[:, 2:3, :] * mix, lg_ref[...], lb_ref[...])
    x1_ref[...] = x1
    xm2 = (x1 * (1.0 + mod_ref[:, 4:5, :]) + mod_ref[:, 3:4, :]).reshape(r, D_MODEL)
    xpk_ref[...] = _pack_bf16_pair(xm2[:, :HALF], xm2[:, HALF:])

    a1, a2, _ = _split3(xm2)
    lg = _dot(a1, wr_ref[0]) + (_dot(a2, wr_ref[0]) + _dot(a1, wr_ref[1]))
    lane = lax.broadcasted_iota(jnp.int32, (r, LANES), 1).astype(f32)
    neg, big = -1e30, 1e6
    mask_g = lane < 4.0
    lgg = jnp.where(mask_g, lg, neg)
    mg = jnp.max(lgg, axis=-1, keepdims=True)
    grp = jnp.min(jnp.where(lgg == mg, lane, big), axis=-1, keepdims=True)
    p_grp = 1.0 / jnp.sum(jnp.where(mask_g, jnp.exp(lgg - mg), 0.0), axis=-1, keepdims=True)
    lo = 4.0 + grp * E_PER_GROUP
    mask_e = (lane >= lo) & (lane < lo + E_PER_GROUP)
    lge = jnp.where(mask_e, lg, neg)
    v1 = jnp.max(lge, axis=-1, keepdims=True)
    i1 = jnp.min(jnp.where(lge == v1, lane, big), axis=-1, keepdims=True)
    lge2 = jnp.where(lane == i1, neg, lge)
    v2 = jnp.max(lge2, axis=-1, keepdims=True)
    i2 = jnp.min(jnp.where(lge2 == v2, lane, big), axis=-1, keepdims=True)
    e21 = jnp.exp(v2 - v1)
    w1 = p_grp / (1.0 + e21)
    w2 = p_grp * e21 / (1.0 + e21)
    rt = jnp.where(lane == 0.0, i1 - 4.0,
                   jnp.where(lane == 1.0, i2 - 4.0,
                             jnp.where(lane == 2.0, w1, jnp.where(lane == 3.0, w2, 0.0))))
    rt_ref[...] = rt


MERGE_WEIGHTS = ("w_gl", "w_br", "w_o", "ln1_g", "ln1_b", "w_router")


def _merge_call(x, ada, ycat, w, l, *, bb, lb, mod_off):
    b, seq, _ = x.shape
    r = bb * lb
    nl = seq // lb
    n = b * seq
    weights = [w[k] for k in MERGE_WEIGHTS]
    return pl.pallas_call(
        functools.partial(_merge_kernel, bb=bb, lb=lb), grid=(b // bb, nl),
        in_specs=[pl.BlockSpec((bb, lb, D_MODEL), lambda i, j: (i, j, 0)),
                  pl.BlockSpec((None, bb, 6, D_MODEL), lambda i, j: (l, mod_off // bb + i, 0, 0)),
                  pl.BlockSpec((r, Y_WIDTH), lambda i, j: (i * nl + j, 0))] + [_layer_spec(a, l) for a in weights],
        out_specs=(pl.BlockSpec((bb, lb, D_MODEL), lambda i, j: (i, j, 0)),
                   pl.BlockSpec((r, HALF), lambda i, j: (i * nl + j, 0)),
                   pl.BlockSpec((r, LANES), lambda i, j: (i * nl + j, 0))),
        out_shape=(jax.ShapeDtypeStruct((b, seq, D_MODEL), f32),
                   jax.ShapeDtypeStruct((n, HALF), jnp.uint32),
                   jax.ShapeDtypeStruct((n, LANES), f32)),
        compiler_params=_params(),
    )(x, ada, ycat, *weights)


def _rank_kernel(rt_ref, dest_ref, cnt_ref, carry, start, *, t, n_tiles):
    p = pl.program_id(1)
    j = pl.program_id(2)
    rt = rt_ref[...]
    lane = lax.broadcasted_iota(jnp.int32, (t, LANES), 1).astype(f32)
    oh0 = jnp.where(lane == rt[:, 0:1], 1.0, 0.0)
    oh1 = jnp.where(lane == rt[:, 1:2], 1.0, 0.0)
    both = oh0 + oh1
    colsum = jnp.sum(both, axis=0, keepdims=True)

    @pl.when((p == 0) & (j == 0))
    def _():
        carry[...] = jnp.zeros(carry.shape, f32)

    @pl.when(p == 0)
    def _():
        carry[0:1, :] = carry[0:1, :] + colsum

    @pl.when((p == 0) & (j == n_tiles - 1))
    def _():
        c = jnp.broadcast_to(carry[0:1, :], (SUBLANES, LANES))
        cnt_ref[0] = c
        nblk = jnp.floor((c + (MOE_BLOCK - 1.0)) * (1.0 / MOE_BLOCK))
        ri = lax.broadcasted_iota(jnp.int32, (LANES, LANES), 0)
        ci = lax.broadcasted_iota(jnp.int32, (LANES, LANES), 1)
        tri = jnp.where(ri < ci, 1.0, 0.0).astype(bf16)
        start[...] = _dot(nblk.astype(bf16), tri) * float(MOE_BLOCK)
        carry[...] = jnp.zeros(carry.shape, f32)

    @pl.when(p == 1)
    def _():
        rr = lax.broadcasted_iota(jnp.int32, (t, t), 0)
        cc = lax.broadcasted_iota(jnp.int32, (t, t), 1)
        ltri = jnp.where(cc < rr, 1.0, 0.0).astype(bf16)
        base = start[0:1, :] + carry[0:1, :] + _dot(ltri, both.astype(bf16))
        d0 = jnp.sum(oh0 * base, axis=-1, keepdims=True)
        d1 = jnp.sum(oh1 * base, axis=-1, keepdims=True)
        arr = jnp.where(lane == 0.0, d0, jnp.where(lane == 1.0, d1, 0.0))
        dest_ref[0] = arr.T[0:2, :].astype(jnp.int32)
        carry[0:1, :] = carry[0:1, :] + colsum


def _rank_call(route, *, n_groups, gsz):
    t = min(RANK_TILE, gsz)
    n_tiles = gsz // t
    return pl.pallas_call(
        functools.partial(_rank_kernel, t=t, n_tiles=n_tiles),
        grid=(n_groups, 2, n_tiles),
        in_specs=[pl.BlockSpec((t, LANES), lambda g, p, j: (g * n_tiles + j, 0))],
        out_specs=(pl.BlockSpec((1, 2, t), lambda g, p, j: (g, 0, j * p)),
                   pl.BlockSpec((1, SUBLANES, LANES), lambda g, p, j: (g, 0, 0))),
        out_shape=(jax.ShapeDtypeStruct((n_groups, 2, gsz), jnp.int32),
                   jax.ShapeDtypeStruct((n_groups, SUBLANES, LANES), f32)),
        scratch_shapes=[pltpu.VMEM((SUBLANES, LANES), f32), pltpu.VMEM((SUBLANES, LANES), f32)],
        compiler_params=_params(3),
    )(route)


def _expert_kernel(be_ref, nv_ref, dest_ref, x_ref, wg_ref, wu_ref, wd_ref, o_ref, xg, yg, tbl,
                   *, gsz, max_blocks):
    g = pl.program_id(0)
    b = pl.program_id(1)
    pad_row = 2 * gsz

    nv = nv_ref[g]
    xgs = (xg.at[0], xg.at[1])
    ygs = (yg.at[0], yg.at[1])

    @pl.when(b == 0)
    def _():
        def init(i, c):
            tbl[i] = pad_row
            return c

        lax.fori_loop(0, max_blocks * MOE_BLOCK, init, 0, unroll=8)

        def fill(t, c):
            tbl[dest_ref[0, 0, t]] = t
            tbl[dest_ref[0, 1, t]] = gsz + t
            return c

        lax.fori_loop(0, gsz, fill, 0, unroll=8)
        o_ref[0, pad_row:pad_row + SUBLANES, :] = jnp.zeros((SUBLANES, HALF), jnp.uint32)
        yg[...] = jnp.zeros(yg.shape, jnp.uint32)

        def gather0(i, c):
            xg[0, pl.ds(i, 1), :] = x_ref[pl.ds(tbl[i] & (gsz - 1), 1), :]
            return c

        lax.fori_loop(0, MOE_BLOCK, gather0, 0, unroll=8)

    def step(cur):
        x_cur, x_nxt, y_cur, y_prv = xgs[cur], xgs[1 - cur], ygs[cur], ygs[1 - cur]
        base_n = jnp.minimum(b + 1, nv - 1) * MOE_BLOCK
        base_p = jnp.maximum(b - 1, 0) * MOE_BLOCK
        first = b == 0
        for i in range(MOE_BLOCK):
            x_nxt[i:i + 1, :] = x_ref[pl.ds(tbl[base_n + i] & (gsz - 1), 1), :]
            s = jnp.where(first, pad_row, tbl[base_p + i])
            o_ref[0, pl.ds(s, 1), :] = y_prv[i:i + 1, :]
        hi, lo = _unpack_bf16_pair(x_cur[...])
        hi = hi.astype(bf16)
        lo = lo.astype(bf16)
        hg = _dot(hi, wg_ref[:HALF, :]) + _dot(lo, wg_ref[HALF:, :])
        hu = _dot(hi, wu_ref[:HALF, :]) + _dot(lo, wu_ref[HALF:, :])
        hid = (_silu(hg) * hu).astype(bf16)
        y = _dot(hid, wd_ref[...])
        y_cur[...] = _pack_bf16_pair(y[:, :HALF], y[:, HALF:])

    @pl.when((b <= nv) & (b % 2 == 0))
    def _():
        step(0)

    @pl.when((b <= nv) & (b % 2 == 1))
    def _():
        step(1)


def _expert_call(xpk, block_e, n_valid, dest, w, l, *, n_groups, gsz, max_blocks):
    assert gsz & (gsz - 1) == 0

    def w_spec(shape):
        return pl.BlockSpec((None, None) + shape,
                            lambda g, b, be, nv: (l, be[g * max_blocks + jnp.minimum(b, max_blocks - 1)], 0, 0))

    single = pl.Buffered(1)
    grid_spec = pltpu.PrefetchScalarGridSpec(
        num_scalar_prefetch=2,
        grid=(n_groups, max_blocks + 1),
        in_specs=[
            pl.BlockSpec((1, 2, gsz), lambda g, b, be, nv: (g, 0, 0), memory_space=pltpu.SMEM),
            pl.BlockSpec((gsz, HALF), lambda g, b, be, nv: (g, 0), pipeline_mode=single),
            w_spec((D_MODEL, EXPERT_FF)), w_spec((D_MODEL, EXPERT_FF)), w_spec((EXPERT_FF, D_MODEL)),
        ],
        out_specs=pl.BlockSpec((1, 2 * gsz + SUBLANES, HALF), lambda g, b, be, nv: (g, 0, 0),
                               pipeline_mode=single),
        scratch_shapes=[pltpu.VMEM((2, MOE_BLOCK, HALF), jnp.uint32), pltpu.VMEM((2, MOE_BLOCK, HALF), jnp.uint32),
                        pltpu.SMEM((max_blocks * MOE_BLOCK,), jnp.int32)],
    )
    return pl.pallas_call(
        functools.partial(_expert_kernel, gsz=gsz, max_blocks=max_blocks), grid_spec=grid_spec,
        out_shape=jax.ShapeDtypeStruct((n_groups, 2 * gsz + SUBLANES, HALF), jnp.uint32),
        compiler_params=_params(),
    )(block_e, n_valid, dest, xpk, w["w_e_gate"], w["w_e_up"], w["w_e_down"])


def _combine_kernel(x_ref, mod_ref, y0_ref, y1_ref, rt_ref, lg_ref, lb_ref, o_ref, *, bb, lb):
    rt = rt_ref[...]
    w0 = rt[:, 2:3]
    w1 = rt[:, 3:4]
    h0, l0 = _unpack_bf16_pair(y0_ref[0])
    h1, l1 = _unpack_bf16_pair(y1_ref[0])
    f = jnp.concatenate([w0 * h0 + w1 * h1, w0 * l0 + w1 * l1], axis=1).reshape(bb, lb, D_MODEL)
    t = DN_ALPHA * x_ref[...] + mod_ref[:, 5:6, :] * f
    o_ref[...] = _layer_norm(t, lg_ref[...], lb_ref[...])


def _combine_call(x1, ada, ys, route, w, l, *, bb, lb, gsz, mod_off):
    b, seq, _ = x1.shape
    r = bb * lb
    nl = seq // lb
    tiles_per_group = gsz // r

    def y_map(k):
        def m(i, j):
            t = i * nl + j
            return (t // tiles_per_group, k * tiles_per_group + t % tiles_per_group, 0)
        return m

    return pl.pallas_call(
        functools.partial(_combine_kernel, bb=bb, lb=lb),
        grid=(b // bb, nl),
        in_specs=[pl.BlockSpec((bb, lb, D_MODEL), lambda i, j: (i, j, 0)),
                  pl.BlockSpec((None, bb, 6, D_MODEL), lambda i, j: (l, mod_off // bb + i, 0, 0)),
                  pl.BlockSpec((1, r, HALF), y_map(0)),
                  pl.BlockSpec((1, r, HALF), y_map(1)),
                  pl.BlockSpec((r, LANES), lambda i, j: (i * nl + j, 0)),
                  _layer_spec(w["ln2_g"], l), _layer_spec(w["ln2_b"], l)],
        out_specs=pl.BlockSpec((bb, lb, D_MODEL), lambda i, j: (i, j, 0)),
        out_shape=jax.ShapeDtypeStruct((b, seq, D_MODEL), f32),
        compiler_params=_params(),
    )(x1, ada, ys, ys, route, w["ln2_g"], w["ln2_b"])


def _moe(x1, ada, xpk, route, w, l, *, bb, lb, gsz, mod_off):
    n = xpk.shape[0]
    n_groups = n // gsz
    max_blocks = (2 * gsz + N_EXPERTS * (MOE_BLOCK - 1)) // MOE_BLOCK
    dest, counts = _rank_call(route, n_groups=n_groups, gsz=gsz)
    cnt = counts[:, 0, :N_EXPERTS].astype(jnp.int32)
    end_blk = jnp.cumsum((cnt + MOE_BLOCK - 1) // MOE_BLOCK, axis=-1)
    n_valid = end_blk[:, -1]
    blk = jnp.arange(max_blocks, dtype=jnp.int32)
    block_e = jnp.sum(blk[None, :, None] >= end_blk[:, None, :], axis=-1).astype(jnp.int32)
    block_e = jnp.minimum(block_e, N_EXPERTS - 1).reshape(n_groups * max_blocks)
    ys = _expert_call(xpk, block_e, n_valid, dest, w, l, n_groups=n_groups, gsz=gsz, max_blocks=max_blocks)
    return _combine_call(x1, ada, ys, route, w, l, bb=bb, lb=lb, gsz=gsz, mod_off=mod_off)


def _win_kernel(w_ref, wa_ref, wgl_ref):
    wa_ref[0] = w_ref[0, :, 0:WA_WIDTH].astype(bf16)
    wgl_ref[0] = w_ref[0, :, GATE_OFF:GATE_OFF + N_BRANCH * D_MODEL].astype(bf16)


def _win_call(w_in):
    rows = 128
    width = w_in.shape[-1]
    return pl.pallas_call(
        _win_kernel, grid=(DEPTH, D_MODEL // rows),
        in_specs=[pl.BlockSpec((1, rows, width), lambda l, i: (l, i, 0))],
        out_specs=(pl.BlockSpec((1, rows, WA_WIDTH), lambda l, i: (l, i, 0)),
                   pl.BlockSpec((1, rows, N_BRANCH * D_MODEL), lambda l, i: (l, i, 0))),
        out_shape=(jax.ShapeDtypeStruct((DEPTH, D_MODEL, WA_WIDTH), bf16),
                   jax.ShapeDtypeStruct((DEPTH, D_MODEL, N_BRANCH * D_MODEL), bf16)),
        compiler_params=_params(),
    )(w_in)


def _prep_weights(p):
    wa, w_gl = _win_call(p["w_in"])
    lane_pad = lambda v: jnp.pad(v, ((0, 0), (0, LANES - v.shape[1])))[:, None, :]
    row = lambda v: v[:, None, :]
    wr = jnp.pad(jnp.concatenate([p["router_g"], p["router_e"]], axis=-1),
                 ((0, 0), (0, 0), (0, LANES - 4 - N_EXPERTS)))
    wr_hi = wr.astype(bf16)
    wr_lo = (wr - wr_hi.astype(f32)).astype(bf16)
    return dict(
        wa=wa, w_gl=w_gl,
        sgu_ln_g=row(p["sgu_ln_g"]), sgu_ln_b=row(p["sgu_ln_b"]),
        conv_w=p["conv_w"], conv_bias=row(p["conv_bias"]),
        conv_ln_g=row(p["conv_ln_g"]), conv_ln_b=row(p["conv_ln_b"]),
        pool_w=p["pool_w"], pool_scale=row(p["pool_scale"]),
        ssm_conv_w=p["ssm_conv_w"], ssm_conv_b=row(p["ssm_conv_b"]),
        dt_bias=lane_pad(p["ssm_dt_bias"]), a_log=lane_pad(p["ssm_a_log"]),
        d_skip=row(jnp.repeat(p["ssm_d"], SSM_HEADDIM, axis=-1)), norm_g=row(p["ssm_norm_g"]),
        w_br=jnp.concatenate([p["w_br_a"], p["w_br_b"], p["w_br_c"], p["w_br_d"]], axis=1).astype(bf16),
        w_o=p["w_o"].astype(bf16),
        ln1_g=row(p["ln1_g"]), ln1_b=row(p["ln1_b"]),
        w_router=jnp.stack([wr_hi, wr_lo], axis=1),
        w_e_gate=p["w_e_gate"].astype(bf16), w_e_up=p["w_e_up"].astype(bf16), w_e_down=p["w_e_down"].astype(bf16),
        ln2_g=row(p["ln2_g"]), ln2_b=row(p["ln2_b"]),
    )


def _prep_sgu(p, q, r):
    nb = r // q
    sgu_w = jnp.tile(p["sgu_w"][:, :, :q, :q], (1, 1, nb, nb))
    bias = jnp.tile(p["sgu_b"][:, :, :q], (1, 1, nb))
    bias = jnp.repeat(jnp.swapaxes(bias, 1, 2), LANES, axis=-1)
    return dict(sgu_w=sgu_w, sgu_bias_rows=bias)


PROMPT_MIX = (1, 128)
SAMPLE_MIX = (8, 8)
PROMPT_TOK = (1, 256)
SAMPLE_TOK = (32, 8)
PROMPT_GROUP = 4096
SAMPLE_GROUP = 1024


def kernel(x_prompt, x_sample, state_conv, state_pool, state_ssm_conv, state_ssm, c_prompt, c_sample,
           w_ada, b_ada, w_in, sgu_ln_g, sgu_ln_b, sgu_w, sgu_b, conv_w, conv_bias, conv_ln_g, conv_ln_b,
           pool_w, pool_scale, ssm_conv_w, ssm_conv_b, ssm_dt_bias, ssm_a_log, ssm_d, ssm_norm_g,
           w_br_a, w_br_b, w_br_c, w_br_d, w_o, ln1_g, ln1_b, router_g, router_e, w_e_gate, w_e_up,
           w_e_down, ln2_g, ln2_b):
    return _forward(x_prompt, x_sample, state_conv, state_pool, state_ssm_conv, state_ssm, c_prompt, c_sample,
                    w_ada, b_ada, w_in, sgu_ln_g, sgu_ln_b, sgu_w, sgu_b, conv_w, conv_bias, conv_ln_g, conv_ln_b,
                    pool_w, pool_scale, ssm_conv_w, ssm_conv_b, ssm_dt_bias, ssm_a_log, ssm_d, ssm_norm_g,
                    w_br_a, w_br_b, w_br_c, w_br_d, w_o, ln1_g, ln1_b, router_g, router_e, w_e_gate, w_e_up,
                    w_e_down, ln2_g, ln2_b)


def _forward(x_prompt, x_sample, state_conv, state_pool, state_ssm_conv, state_ssm, c_prompt, c_sample,
             w_ada, b_ada, w_in, sgu_ln_g, sgu_ln_b, sgu_w, sgu_b, conv_w, conv_bias, conv_ln_g, conv_ln_b,
             pool_w, pool_scale, ssm_conv_w, ssm_conv_b, ssm_dt_bias, ssm_a_log, ssm_d, ssm_norm_g,
             w_br_a, w_br_b, w_br_c, w_br_d, w_o, ln1_g, ln1_b, router_g, router_e, w_e_gate, w_e_up,
             w_e_down, ln2_g, ln2_b, prompt_group=PROMPT_GROUP, sample_group=SAMPLE_GROUP,
             sample_tok=SAMPLE_TOK):
    p = dict(w_in=w_in, sgu_ln_g=sgu_ln_g, sgu_ln_b=sgu_ln_b, sgu_w=sgu_w, sgu_b=sgu_b, conv_w=conv_w,
             conv_bias=conv_bias, conv_ln_g=conv_ln_g, conv_ln_b=conv_ln_b, pool_w=pool_w, pool_scale=pool_scale,
             ssm_conv_w=ssm_conv_w, ssm_conv_b=ssm_conv_b, ssm_dt_bias=ssm_dt_bias, ssm_a_log=ssm_a_log,
             ssm_d=ssm_d, ssm_norm_g=ssm_norm_g, w_br_a=w_br_a, w_br_b=w_br_b, w_br_c=w_br_c, w_br_d=w_br_d,
             w_o=w_o, ln1_g=ln1_g, ln1_b=ln1_b, router_g=router_g, router_e=router_e, ln2_g=ln2_g, ln2_b=ln2_b,
             w_e_gate=w_e_gate, w_e_up=w_e_up, w_e_down=w_e_down)
    bp = x_prompt.shape[0]
    bs = x_sample.shape[0]
    w = _prep_weights(p)
    ada = _ada_call(jnp.concatenate([c_sample, c_prompt], axis=0), w_ada, b_ada)
    ada = ada.reshape(DEPTH, bs + bp, 6, D_MODEL)

    groups = [
        dict(x=x_prompt, mix=PROMPT_MIX, tok=PROMPT_TOK, gsz=prompt_group, start=0, mod_off=bs, vn=None,
             states=tuple(jnp.zeros((DEPTH, bp) + s, f32) for s in STATE_SHAPES)),
        dict(x=x_sample, mix=SAMPLE_MIX, tok=sample_tok, gsz=sample_group, start=PAST_LEN, mod_off=0,
             vn=jnp.zeros((DEPTH, bs, x_sample.shape[1], A_WIDTH), f32),
             states=(state_conv, state_pool, state_ssm_conv, state_ssm.reshape(DEPTH, bs, D_WIDTH, SSM_STATE))),
    ]
    for g in groups:
        g["w"] = dict(w, **_prep_sgu(p, g["mix"][1], g["mix"][0] * g["mix"][1]))
    for l in range(DEPTH):
        for g in groups:
            bb, lb = g["mix"]
            res = _mixer_call(g["x"], ada, g["states"], g["vn"], g["w"], l, bb=bb, lb=lb, start_pos=g["start"],
                              mod_off=g["mod_off"])
            g["states"] = tuple(res[1:5])
            if g["vn"] is not None:
                g["vn"] = res[5]
            bb, lb = g["tok"]
            x1, xpk, route = _merge_call(g["x"], ada, res[0], g["w"], l, bb=bb, lb=lb, mod_off=g["mod_off"])
            g["x"] = _moe(x1, ada, xpk, route, g["w"], l, bb=bb, lb=lb, gsz=g["gsz"], mod_off=g["mod_off"])

    def states_out(g, b):
        c, pl_, sc, sm = g["states"]
        return c, pl_, sc, sm.reshape(DEPTH, b, SSM_HEADS, SSM_HEADDIM, SSM_STATE)

    return ((groups[0]["x"], groups[1]["x"]) + states_out(groups[0], bp) + states_out(groups[1], bs)
            + (groups[1]["vn"],))
```

```python
import functools
import math

import jax
import jax.numpy as jnp
import numpy as np
from jax import lax
from jax.experimental import pallas as pl
from jax.experimental.pallas import tpu as pltpu

D_MODEL = 1024
DEPTH = 4
A_WIDTH = 512
A_GROUPS = 4
B_WIDTH = 512
CONV_W = 31
C_WIDTH = 512
POOL_WINDOWS = (2, 4, 8, 16)
POOL_BUF = 15
SSM_HEADS = 12
SSM_HEADDIM = 64
D_WIDTH = SSM_HEADS * SSM_HEADDIM
SSM_GROUPS = 4
SSM_STATE = 128
SSM_CONV = 4
XBC_WIDTH = D_WIDTH + 2 * SSM_GROUPS * SSM_STATE
N_BRANCH = 4
E_PER_GROUP = 8
N_EXPERTS = 32
EXPERT_FF = 512
MOE_BLOCK = 128
DN_ALPHA = (2 * DEPTH) ** 0.25
LN_EPS = 1e-5
PAST_LEN = 16384
Y_WIDTH = A_WIDTH + B_WIDTH + C_WIDTH + D_WIDTH
HALF = D_MODEL // 2

OFF_U, OFF_V, OFF_GA, OFF_GG, OFF_P, OFF_Z, OFF_XBC, OFF_DT = 0, 512, 1024, 1536, 2048, 2560, 3328, 5120
GATE_OFF = OFF_DT + SSM_HEADS
WA_WIDTH = 5248
LANES = 128
SUBLANES = 8
HIST_CONV = 32
HIST_POOL = 16
HIST_SCONV = 8
VMEM_LIMIT = 56 * 1024 * 1024
RANK_TILE = 512

f32 = jnp.float32
bf16 = jnp.bfloat16


def _dot(a, b):
    return jnp.dot(a, b, preferred_element_type=f32)


def _dot_nt(a, b):
    return lax.dot_general(a, b, (((1,), (1,)), ((), ())), preferred_element_type=f32)


def _dot_tn(a, b):
    return lax.dot_general(a, b, (((0,), (0,)), ((), ())), preferred_element_type=f32)


def _split3(x):
    h1 = x.astype(bf16)
    r1 = x - h1.astype(f32)
    h2 = r1.astype(bf16)
    r2 = r1 - h2.astype(f32)
    return h1, h2, r2.astype(bf16)


def _sigmoid(x):
    return 0.5 * (jnp.tanh(0.5 * x) + 1.0)


def _silu(x):
    return x * _sigmoid(x)


def _gelu(x):
    return 0.5 * x * (1.0 + jnp.tanh(math.sqrt(2.0 / math.pi) * (x + 0.044715 * (x * x * x))))


def _softplus(x):
    return jnp.maximum(x, 0.0) + jnp.log1p(jnp.exp(-jnp.abs(x)))


def _layer_norm(x, g, b):
    mu = jnp.mean(x, axis=-1, keepdims=True)
    xc = x - mu
    var = jnp.mean(xc * xc, axis=-1, keepdims=True)
    return xc * lax.rsqrt(var + LN_EPS) * g + b


def _layer_spec(a, l):
    nd = a.ndim
    return pl.BlockSpec((None,) + a.shape[1:], lambda *_: (l,) + (0,) * (nd - 1))


def _params(n_axes=2):
    return pltpu.CompilerParams(dimension_semantics=("arbitrary",) * n_axes, vmem_limit_bytes=VMEM_LIMIT)


def _ada_kernel(c_ref, w_ref, b_ref, o_ref):
    a = _silu(c_ref[...]).astype(bf16)
    o_ref[0] = _dot(a, w_ref[0].astype(bf16)) + b_ref[0]


def _ada_call(c_all, w_ada, b_ada):
    n = c_all.shape[0]
    return pl.pallas_call(
        _ada_kernel,
        grid=(DEPTH, 6),
        in_specs=[
            pl.BlockSpec((n, D_MODEL), lambda l, j: (0, 0)),
            pl.BlockSpec((1, D_MODEL, D_MODEL), lambda l, j: (l, 0, j)),
            pl.BlockSpec((1, 1, D_MODEL), lambda l, j: (l, 0, j)),
        ],
        out_specs=pl.BlockSpec((1, n, D_MODEL), lambda l, j: (l, 0, j)),
        out_shape=jax.ShapeDtypeStruct((DEPTH, n, 6 * D_MODEL), f32),
        compiler_params=_params(),
    )(c_all, w_ada, b_ada.reshape(DEPTH, 1, 6 * D_MODEL))


def _mixer_kernel(x_ref, mod_ref, stc_ref, stp_ref, sts_ref, stm_ref, *rest, bb, lb, start_pos, emit_vn):
    if emit_vn:
        rest = rest[1:]
    (wa_ref, sgw_ref, sgb_ref, slg_ref, slb_ref, cw_ref, cb_ref, clg_ref, clb_ref, pw_ref, ps_ref,
     scw_ref, scb_ref, dtb_ref, alog_ref, dsk_ref, ng_ref, y_ref, nc_ref, np_ref, ns_ref, nm_ref) = rest[:22]
    vn_ref = rest[22] if emit_vn else None
    hbuf, hsh, pbuf, xbuf, sst, cbuf = rest[-6:]
    r = bb * lb
    nb, q = bb, lb
    lq = q.bit_length() - 1
    j = pl.program_id(1)

    @pl.when(j == 0)
    def _():
        hbuf[:, HIST_CONV - (CONV_W - 1):HIST_CONV, :] = stc_ref[...]
        pbuf[:, HIST_POOL - POOL_BUF:HIST_POOL, :] = stp_ref[...]
        xbuf[:, HIST_SCONV - (SSM_CONV - 1):HIST_SCONV, :] = sts_ref[...]
        sst[...] = stm_ref[...]

    x = x_ref[...]
    xm = (x * (1.0 + mod_ref[:, 1:2, :]) + mod_ref[:, 0:1, :]).reshape(r, D_MODEL).astype(bf16)

    row_i = lax.broadcasted_iota(jnp.int32, (r, r), 0)
    col_i = lax.broadcasted_iota(jnp.int32, (r, r), 1)
    same = (row_i >> lq) == (col_i >> lq)
    causal = same & ((col_i & (q - 1)) <= (row_i & (q - 1)))
    causal_t = same & ((row_i & (q - 1)) <= (col_i & (q - 1)))

    u = _gelu(_dot(xm, wa_ref[:, OFF_U:OFF_U + A_WIDTH]))
    v = _gelu(_dot(xm, wa_ref[:, OFF_V:OFF_V + A_WIDTH]))
    vn = _layer_norm(v, slg_ref[...], slb_ref[...])
    if emit_vn:
        vn_ref[...] = vn.reshape(bb, lb, A_WIDTH)
    vnb = vn.astype(bf16)
    for g in range(A_GROUPS):
        sl = slice(g * LANES, (g + 1) * LANES)
        s = _dot(sgw_ref[g], vnb[:, sl]) + sgb_ref[:, sl]
        y_ref[:, sl] = (u[:, sl] * s).astype(bf16)

    ga = _dot(xm, wa_ref[:, OFF_GA:OFF_GA + B_WIDTH])
    gg = _dot(xm, wa_ref[:, OFF_GG:OFF_GG + B_WIDTH])
    h = ga * _sigmoid(gg)
    hbuf[:, HIST_CONV:HIST_CONV + lb, :] = h.reshape(bb, lb, B_WIDTH)
    base = HIST_CONV - (CONV_W - 1)
    span = HIST_CONV + lb - SUBLANES
    for s in range(1, SUBLANES):
        hsh[s - 1, :, 0:span, :] = hbuf[:, s:s + span, :]
    lc = min(lb, 64)
    for rc in range(lb // lc):
        acc = jnp.zeros((bb, lc, B_WIDTH), f32) + cb_ref[...]
        for k in range(CONV_W):
            o = base + k + rc * lc
            s, a = o % SUBLANES, o - o % SUBLANES
            tap = hbuf[:, a:a + lc, :] if s == 0 else hsh[s - 1, :, a:a + lc, :]
            acc = acc + tap * cw_ref[k:k + 1, :]
        cbuf[:, rc * lc:(rc + 1) * lc, :] = acc
    yb = _silu(_layer_norm(cbuf[...].reshape(r, B_WIDTH), clg_ref[...], clb_ref[...]))
    y_ref[:, A_WIDTH:A_WIDTH + B_WIDTH] = yb.astype(bf16)
    new_c = hbuf[:, lb + base:lb + HIST_CONV, :]
    nc_ref[...] = new_c
    hbuf[:, base:HIST_CONV, :] = new_c

    pin = _dot(xm, wa_ref[:, OFF_P:OFF_P + C_WIDTH])
    pbuf[:, HIST_POOL:HIST_POOL + lb, :] = pin.reshape(bb, lb, C_WIDTH)
    pos = (start_pos + j * lb + lax.broadcasted_iota(jnp.int32, (bb, lb, LANES), 1)).astype(f32)
    for gi, win in enumerate(POOL_WINDOWS):
        sl = slice(gi * LANES, (gi + 1) * LANES)
        acc = pbuf[:, HIST_POOL:HIST_POOL + lb, sl]
        for i in range(1, win):
            acc = acc + pbuf[:, HIST_POOL - i:HIST_POOL - i + lb, sl]
        cnt = jnp.minimum(float(win), pos + 1.0)
        mixed = (acc / cnt - pbuf[:, HIST_POOL:HIST_POOL + lb, sl]).reshape(r, LANES)
        out = _dot(mixed.astype(bf16), pw_ref[gi].astype(bf16)) * ps_ref[:, sl]
        y_ref[:, A_WIDTH + B_WIDTH + gi * LANES:A_WIDTH + B_WIDTH + (gi + 1) * LANES] = out.astype(bf16)
    new_p = pbuf[:, lb + HIST_POOL - POOL_BUF:lb + HIST_POOL, :]
    np_ref[...] = new_p
    pbuf[:, HIST_POOL - POOL_BUF:HIST_POOL, :] = new_p

    z = _dot(xm, wa_ref[:, OFF_Z:OFF_Z + D_WIDTH])
    xbc = _dot(xm, wa_ref[:, OFF_XBC:OFF_XBC + XBC_WIDTH])
    dtr = _dot(xm, wa_ref[:, OFF_DT:OFF_DT + LANES])
    xbuf[:, HIST_SCONV:HIST_SCONV + lb, :] = xbc.reshape(bb, lb, XBC_WIDTH)
    sbase = HIST_SCONV - (SSM_CONV - 1)
    acc = jnp.zeros((bb, lb, XBC_WIDTH), f32) + scb_ref[...]
    for k in range(SSM_CONV):
        acc = acc + xbuf[:, sbase + k:sbase + k + lb, :] * scw_ref[k:k + 1, :]
    xc = _silu(acc).reshape(r, XBC_WIDTH)
    new_s = xbuf[:, lb + sbase:lb + HIST_SCONV, :]
    ns_ref[...] = new_s
    xbuf[:, sbase:HIST_SCONV, :] = new_s

    dt = _softplus(dtr + dtb_ref[...])
    da = dt * (-jnp.exp(alog_ref[...]))
    d1, d2, d3 = _split3(da)
    mc = jnp.where(causal, 1.0, 0.0).astype(bf16)
    mct = jnp.where(causal_t, 1.0, 0.0).astype(bf16)
    ms = jnp.where(same, 1.0, 0.0).astype(bf16)
    acs = _dot(mc, d1) + _dot(mc, d2) + _dot(mc, d3)
    acs_t = _dot_tn(d1, mct) + _dot_tn(d2, mct) + _dot_tn(d3, mct)
    tot = _dot(ms, d1) + _dot(ms, d2) + _dot(ms, d3)

    lane = lax.broadcasted_iota(jnp.int32, (r, LANES), 1)
    lo_half = lane < SSM_HEADDIM
    cbs = []
    for g in range(SSM_GROUPS):
        bg = xc[:, D_WIDTH + g * SSM_STATE:D_WIDTH + (g + 1) * SSM_STATE].astype(bf16)
        cg = xc[:, D_WIDTH + (SSM_GROUPS + g) * SSM_STATE:D_WIDTH + (SSM_GROUPS + g + 1) * SSM_STATE].astype(bf16)
        cbs.append((bg, cg, _dot_nt(cg, bg)))
    rep = SSM_HEADS // SSM_GROUPS
    rowseq = lax.broadcasted_iota(jnp.int32, (r, LANES), 0) >> lq
    ysq = jnp.zeros((r, 1), f32)
    ypairs = []
    for i in range(SSM_HEADS // 2):
        h0, h1 = 2 * i, 2 * i + 1
        g0, g1 = h0 // rep, h1 // rep
        sl = slice(i * LANES, (i + 1) * LANES)
        xs = xc[:, sl]
        dtp = jnp.where(lo_half, dt[:, h0:h0 + 1], dt[:, h1:h1 + 1])
        acp = jnp.where(lo_half, acs[:, h0:h0 + 1], acs[:, h1:h1 + 1])
        totp = jnp.where(lo_half, tot[:, h0:h0 + 1], tot[:, h1:h1 + 1])
        xd = xs * dtp
        xdb = xd.astype(bf16)
        res = []
        for hh, gg_ in ((h0, g0), (h1, g1)):
            diff = acs[:, hh:hh + 1] - acs_t[hh:hh + 1, :]
            dec = jnp.exp(jnp.where(causal, diff, -1e30))
            res.append(_dot((cbs[gg_][2] * dec).astype(bf16), xdb))
        ydiag = jnp.where(lo_half, res[0], res[1])
        s_all = sst[:, i * LANES:(i + 1) * LANES, :].reshape(nb * LANES, SSM_STATE)
        s_bf = s_all.astype(bf16)
        full0 = _dot_nt(cbs[g0][1], s_bf)
        full1 = full0 if g1 == g0 else _dot_nt(cbs[g1][1], s_bf)
        yoff = jnp.zeros((r, LANES), f32)
        for b in range(nb):
            blk = jnp.where(lo_half, full0[:, b * LANES:(b + 1) * LANES], full1[:, b * LANES:(b + 1) * LANES])
            yoff = yoff + (jnp.where(rowseq == b, blk, 0.0) if nb > 1 else blk)
        yp = ydiag + yoff * jnp.exp(acp) + dsk_ref[:, sl] * xs
        yp = yp * _silu(z[:, sl])
        ysq = ysq + jnp.sum(yp * yp, axis=-1, keepdims=True)
        ypairs.append(yp)
        xdd = xd * jnp.exp(totp - acp)
        if nb > 1:
            colblk = lax.broadcasted_iota(jnp.int32, (r, nb * LANES), 1) >> 7
            rowblk = lax.broadcasted_iota(jnp.int32, (r, nb * LANES), 0) >> lq
            xblk = jnp.where(colblk == rowblk, jnp.concatenate([xdd] * nb, axis=1), 0.0).astype(bf16)
        else:
            xblk = xdd.astype(bf16)
        upd0 = _dot_tn(xblk, cbs[g0][0])
        if g1 != g0:
            upd1 = _dot_tn(xblk, cbs[g1][0])
            prow = lax.broadcasted_iota(jnp.int32, (nb * LANES, SSM_STATE), 0) & (LANES - 1)
            upd0 = jnp.where(prow < SSM_HEADDIM, upd0, upd1)
        upd0 = upd0.reshape(nb, LANES, SSM_STATE)
        for b in range(nb):
            for hh, half in ((h0, 0), (h1, 1)):
                fac = jnp.exp(jnp.broadcast_to(tot[b * q:b * q + 1, hh:hh + 1], (SSM_HEADDIM, SSM_STATE)))
                rs = slice(i * LANES + half * SSM_HEADDIM, i * LANES + (half + 1) * SSM_HEADDIM)
                sst[b, rs, :] = fac * sst[b, rs, :] + upd0[b, half * SSM_HEADDIM:(half + 1) * SSM_HEADDIM, :]
    rms = lax.rsqrt(ysq * (1.0 / D_WIDTH) + LN_EPS)
    for i in range(SSM_HEADS // 2):
        sl = slice(i * LANES, (i + 1) * LANES)
        yd = ypairs[i] * rms * ng_ref[:, sl]
        o = A_WIDTH + B_WIDTH + C_WIDTH + i * LANES
        y_ref[:, o:o + LANES] = yd.astype(bf16)
    nm_ref[...] = sst[...]


def _merge_kernel(x_ref, mod_ref, y_ref, wg_ref, wbr_ref, wo_ref, lg_ref, lb_ref, wr_ref,
                  x1_ref, xpk_ref, rt_ref, *, bb, lb):
    r = bb * lb
    x = x_ref[...]
    xm = (x * (1.0 + mod_ref[:, 1:2, :]) + mod_ref[:, 0:1, :]).reshape(r, D_MODEL).astype(bf16)
    y = y_ref[...]
    offs = (0, A_WIDTH, A_WIDTH + B_WIDTH, A_WIDTH + B_WIDTH + C_WIDTH, Y_WIDTH)
    merged = jnp.zeros((r, D_MODEL), f32)
    for k in range(N_BRANCH):
        gate = _sigmoid(_dot(xm, wg_ref[:, k * D_MODEL:(k + 1) * D_MODEL]))
        merged = merged + gate * _dot(y[:, offs[k]:offs[k + 1]], wbr_ref[offs[k]:offs[k + 1], :])
    mix = _dot(merged.astype(bf16), wo_ref[...]).reshape(bb, lb, D_MODEL)
    x1 = _layer_norm(DN_ALPHA * x + mod_ref[:, 2:3, :] * mix, lg_ref[...], lb_ref[...])
    x1_ref[...] = x1
    xm2 = (x1 * (1.0 + mod_ref[:, 4:5, :]) + mod_ref[:, 3:4, :]).reshape(r, D_MODEL)
    xpk_ref[...] = _pack_bf16_pair(xm2[:, :HALF], xm2[:, HALF:])
    rt_ref[...] = _route(xm2, wr_ref)


def _route(xm2, wr_ref):
    r = xm2.shape[0]
    a1, a2, _ = _split3(xm2)
    lg = _dot(a1, wr_ref[0]) + (_dot(a2, wr_ref[0]) + _dot(a1, wr_ref[1]))
    lane = lax.broadcasted_iota(jnp.int32, (r, LANES), 1).astype(f32)
    neg, big = -1e30, 1e6
    mask_g = lane < 4.0
    lgg = jnp.where(mask_g, lg, neg)
    mg = jnp.max(lgg, axis=-1, keepdims=True)
    grp = jnp.min(jnp.where(lgg == mg, lane, big), axis=-1, keepdims=True)
    p_grp = 1.0 / jnp.sum(jnp.where(mask_g, jnp.exp(lgg - mg), 0.0), axis=-1, keepdims=True)
    lo = 4.0 + grp * E_PER_GROUP
    mask_e = (lane >= lo) & (lane < lo + E_PER_GROUP)
    lge = jnp.where(mask_e, lg, neg)
    v1 = jnp.max(lge, axis=-1, keepdims=True)
    i1 = jnp.min(jnp.where(lge == v1, lane, big), axis=-1, keepdims=True)
    lge2 = jnp.where(lane == i1, neg, lge)
    v2 = jnp.max(lge2, axis=-1, keepdims=True)
    i2 = jnp.min(jnp.where(lge2 == v2, lane, big), axis=-1, keepdims=True)
    e21 = jnp.exp(v2 - v1)
    w1 = p_grp / (1.0 + e21)
    w2 = p_grp * e21 / (1.0 + e21)
    return jnp.where(lane == 0.0, i1 - 4.0,
                     jnp.where(lane == 1.0, i2 - 4.0,
                               jnp.where(lane == 2.0, w1, jnp.where(lane == 3.0, w2, 0.0))))


MIXER_WEIGHTS = ("wa", "sgu_w", "sgu_bias_rows", "sgu_ln_g", "sgu_ln_b", "conv_w", "conv_bias", "conv_ln_g",
                 "conv_ln_b", "pool_w", "pool_scale", "ssm_conv_w", "ssm_conv_b", "dt_bias", "a_log", "d_skip",
                 "norm_g")
MERGE_WEIGHTS = ("w_gl", "w_br", "w_o", "ln1_g", "ln1_b", "w_router")
BIG_WEIGHTS = ("wa", "w_gl", "w_br", "w_o")
STATE_SHAPES = ((CONV_W - 1, B_WIDTH), (POOL_BUF, C_WIDTH), (SSM_CONV - 1, XBC_WIDTH), (D_WIDTH, SSM_STATE))


def _weight_spec(w, name, l):
    spec = _layer_spec(w[name], l)
    if name in BIG_WEIGHTS:
        spec = pl.BlockSpec(spec.block_shape, spec.index_map, pipeline_mode=pl.Buffered(1))
    return spec


def _mixer_call(x, ada, states, vn_buf, w, l, *, bb, lb, start_pos, mod_off):
    b, seq, _ = x.shape
    r = bb * lb
    nl = seq // lb
    emit_vn = vn_buf is not None
    kern = functools.partial(_mixer_kernel, bb=bb, lb=lb, start_pos=start_pos, emit_vn=emit_vn)

    def state_spec(shape):
        return pl.BlockSpec((None, bb) + shape, lambda i, j: (l, i) + (0,) * len(shape))

    weights = [w[n] for n in MIXER_WEIGHTS]
    extra_in = [vn_buf] if emit_vn else []
    in_specs = ([pl.BlockSpec((bb, lb, D_MODEL), lambda i, j: (i, j, 0)),
                 pl.BlockSpec((None, bb, 6, D_MODEL), lambda i, j: (l, mod_off // bb + i, 0, 0))]
                + [state_spec(s) for s in STATE_SHAPES]
                + [pl.BlockSpec(memory_space=pl.ANY) for _ in extra_in]
                + [_weight_spec(w, n, l) for n in MIXER_WEIGHTS])
    n_st = len(STATE_SHAPES)
    out_shape = [jax.ShapeDtypeStruct((b * seq, Y_WIDTH), bf16)]
    out_shape += [jax.ShapeDtypeStruct(s.shape, f32) for s in states]
    out_specs = [pl.BlockSpec((r, Y_WIDTH), lambda i, j: (i * nl + j, 0))] + [state_spec(s) for s in STATE_SHAPES]
    aliases = {2 + k: 1 + k for k in range(n_st)}
    if emit_vn:
        out_shape.append(jax.ShapeDtypeStruct(vn_buf.shape, f32))
        out_specs.append(pl.BlockSpec((None, bb, lb, A_WIDTH), lambda i, j: (l, i, j, 0)))
        aliases[2 + n_st] = 1 + n_st
    scratch = [
        pltpu.VMEM((bb, HIST_CONV + lb, B_WIDTH), f32),
        pltpu.VMEM((SUBLANES - 1, bb, HIST_CONV + lb, B_WIDTH), f32),
        pltpu.VMEM((bb, HIST_POOL + lb, C_WIDTH), f32),
        pltpu.VMEM((bb, HIST_SCONV + lb, XBC_WIDTH), f32),
        pltpu.VMEM((bb, D_WIDTH, SSM_STATE), f32),
        pltpu.VMEM((bb, lb, B_WIDTH), f32),
    ]
    res = pl.pallas_call(
        kern, grid=(b // bb, nl), in_specs=in_specs, out_specs=out_specs, out_shape=out_shape,
        scratch_shapes=scratch, input_output_aliases=aliases, compiler_params=_params(),
    )(x, ada, *states, *extra_in, *weights)
    return res[0], tuple(res[1:1 + n_st]), (res[1 + n_st] if emit_vn else None)


def _merge_call(x, ada, ycat, w, l, *, bb, lb, mod_off):
    b, seq, _ = x.shape
    r = bb * lb
    nl = seq // lb
    n = b * seq
    weights = [w[k] for k in MERGE_WEIGHTS]
    return pl.pallas_call(
        functools.partial(_merge_kernel, bb=bb, lb=lb), grid=(b // bb, nl),
        in_specs=[pl.BlockSpec((bb, lb, D_MODEL), lambda i, j: (i, j, 0)),
                  pl.BlockSpec((None, bb, 6, D_MODEL), lambda i, j: (l, mod_off // bb + i, 0, 0)),
                  pl.BlockSpec((r, Y_WIDTH), lambda i, j: (i * nl + j, 0))]
        + [_weight_spec(w, k, l) for k in MERGE_WEIGHTS],
        out_specs=(pl.BlockSpec((bb, lb, D_MODEL), lambda i, j: (i, j, 0)),
                   pl.BlockSpec((r, HALF), lambda i, j: (i * nl + j, 0)),
                   pl.BlockSpec((r, LANES), lambda i, j: (i * nl + j, 0))),
        out_shape=(jax.ShapeDtypeStruct((b, seq, D_MODEL), f32),
                   jax.ShapeDtypeStruct((n, HALF), jnp.uint32),
                   jax.ShapeDtypeStruct((n, LANES), f32)),
        compiler_params=_params(),
    )(x, ada, ycat, *weights)


def _pack_bf16_pair(a, b):
    ua = lax.bitcast_convert_type(a.astype(bf16).astype(f32), jnp.uint32)
    ub = lax.bitcast_convert_type(b.astype(bf16).astype(f32), jnp.uint32)
    return (ua & jnp.uint32(0xFFFF0000)) | (ub >> 16)


def _unpack_bf16_pair(u):
    hi = lax.bitcast_convert_type(u & jnp.uint32(0xFFFF0000), f32)
    lo = lax.bitcast_convert_type(u << 16, f32)
    return hi, lo


def _rank_kernel(rt_ref, dest_ref, cnt_ref, carry, start, *, t, n_tiles):
    p = pl.program_id(1)
    j = pl.program_id(2)
    rt = rt_ref[...]
    lane = lax.broadcasted_iota(jnp.int32, (t, LANES), 1).astype(f32)
    oh0 = jnp.where(lane == rt[:, 0:1], 1.0, 0.0)
    oh1 = jnp.where(lane == rt[:, 1:2], 1.0, 0.0)
    both = oh0 + oh1
    colsum = jnp.sum(both, axis=0, keepdims=True)

    @pl.when((p == 0) & (j == 0))
    def _():
        carry[...] = jnp.zeros(carry.shape, f32)

    @pl.when(p == 0)
    def _():
        carry[0:1, :] = carry[0:1, :] + colsum

    @pl.when((p == 0) & (j == n_tiles - 1))
    def _():
        c = jnp.broadcast_to(carry[0:1, :], (SUBLANES, LANES))
        cnt_ref[0] = c
        nblk = jnp.floor((c + (MOE_BLOCK - 1.0)) * (1.0 / MOE_BLOCK))
        ri = lax.broadcasted_iota(jnp.int32, (LANES, LANES), 0)
        ci = lax.broadcasted_iota(jnp.int32, (LANES, LANES), 1)
        tri = jnp.where(ri < ci, 1.0, 0.0).astype(bf16)
        start[...] = _dot(nblk.astype(bf16), tri) * float(MOE_BLOCK)
        carry[...] = jnp.zeros(carry.shape, f32)

    @pl.when(p == 1)
    def _():
        rr = lax.broadcasted_iota(jnp.int32, (t, t), 0)
        cc = lax.broadcasted_iota(jnp.int32, (t, t), 1)
        ltri = jnp.where(cc < rr, 1.0, 0.0).astype(bf16)
        base = start[0:1, :] + carry[0:1, :] + _dot(ltri, both.astype(bf16))
        d0 = jnp.sum(oh0 * base, axis=-1, keepdims=True)
        d1 = jnp.sum(oh1 * base, axis=-1, keepdims=True)
        arr = jnp.where(lane == 0.0, d0, jnp.where(lane == 1.0, d1, 0.0))
        dest_ref[0] = arr.T[0:2, :].astype(jnp.int32)
        carry[0:1, :] = carry[0:1, :] + colsum


def _rank_call(route, *, n_groups, gsz):
    t = min(RANK_TILE, gsz)
    n_tiles = gsz // t
    return pl.pallas_call(
        functools.partial(_rank_kernel, t=t, n_tiles=n_tiles),
        grid=(n_groups, 2, n_tiles),
        in_specs=[pl.BlockSpec((t, LANES), lambda g, p, j: (g * n_tiles + j, 0))],
        out_specs=(pl.BlockSpec((1, 2, t), lambda g, p, j: (g, 0, j * p)),
                   pl.BlockSpec((1, SUBLANES, LANES), lambda g, p, j: (g, 0, 0))),
        out_shape=(jax.ShapeDtypeStruct((n_groups, 2, gsz), jnp.int32),
                   jax.ShapeDtypeStruct((n_groups, SUBLANES, LANES), f32)),
        scratch_shapes=[pltpu.VMEM((SUBLANES, LANES), f32), pltpu.VMEM((SUBLANES, LANES), f32)],
        compiler_params=_params(3),
    )(route)


def _expert_kernel(be_ref, nv_ref, dest_ref, x_ref, wg0_ref, wu0_ref, wd0_ref, wg1_ref, wu1_ref, wd1_ref,
                   o_ref, xg, yg, tbl, *, gsz, max_blocks):
    g = pl.program_id(0)
    b = pl.program_id(1)
    pad_row = 2 * gsz

    nv = nv_ref[g]
    xgs = tuple(xg.at[k] for k in range(4))
    ygs = tuple(yg.at[k] for k in range(4))
    w_refs = ((wg0_ref, wu0_ref, wd0_ref), (wg1_ref, wu1_ref, wd1_ref))

    @pl.when(b == 0)
    def _():
        def init(i, c):
            tbl[i] = pad_row
            return c

        lax.fori_loop(0, max_blocks * MOE_BLOCK, init, 0, unroll=8)

        def fill(t, c):
            tbl[dest_ref[0, 0, t]] = t
            tbl[dest_ref[0, 1, t]] = gsz + t
            return c

        lax.fori_loop(0, gsz, fill, 0, unroll=8)
        o_ref[0, pad_row:pad_row + SUBLANES, :] = jnp.zeros((SUBLANES, HALF), jnp.uint32)
        yg[...] = jnp.zeros(yg.shape, jnp.uint32)

        def gather0(i, c):
            xg[0, pl.ds(i, 1), :] = x_ref[pl.ds(tbl[i] & (gsz - 1), 1), :]
            second = jnp.minimum(1, nv - 1) * MOE_BLOCK
            xg[1, pl.ds(i, 1), :] = x_ref[pl.ds(tbl[second + i] & (gsz - 1), 1), :]
            return c

        lax.fori_loop(0, MOE_BLOCK, gather0, 0, unroll=8)

    def step(par):
        nxt = 1 - par
        for t in range(2):
            kn = 2 * (b + 1) + t
            kp = 2 * (b - 1) + t
            base_n = jnp.minimum(kn, nv - 1) * MOE_BLOCK
            base_p = jnp.clip(kp, 0, nv - 1) * MOE_BLOCK
            live = (kp >= 0) & (kp < nv)
            x_nxt, y_prv = xgs[2 * nxt + t], ygs[2 * nxt + t]
            for i in range(MOE_BLOCK):
                x_nxt[i:i + 1, :] = x_ref[pl.ds(tbl[base_n + i] & (gsz - 1), 1), :]
                s = jnp.where(live, tbl[base_p + i], pad_row)
                o_ref[0, pl.ds(s, 1), :] = y_prv[i:i + 1, :]
        for t in range(2):
            wg_ref, wu_ref, wd_ref = w_refs[t]
            hi, lo = _unpack_bf16_pair(xgs[2 * par + t][...])
            hi = hi.astype(bf16)
            lo = lo.astype(bf16)
            hg = _dot(hi, wg_ref[:HALF, :]) + _dot(lo, wg_ref[HALF:, :])
            hu = _dot(hi, wu_ref[:HALF, :]) + _dot(lo, wu_ref[HALF:, :])
            hid = (_silu(hg) * hu).astype(bf16)
            y = _dot(hid, wd_ref[...])
            ygs[2 * par + t][...] = _pack_bf16_pair(y[:, :HALF], y[:, HALF:])

    @pl.when((2 * b <= nv + 1) & (b % 2 == 0))
    def _():
        step(0)

    @pl.when((2 * b <= nv + 1) & (b % 2 == 1))
    def _():
        step(1)


def _expert_call(xpk, block_e, n_valid, dest, w, l, *, n_groups, gsz, max_blocks):
    assert gsz & (gsz - 1) == 0

    def w_spec(shape, t):
        return pl.BlockSpec(
            (None, None) + shape,
            lambda g, b, be, nv: (l, be[g * max_blocks + jnp.minimum(2 * b + t, max_blocks - 1)], 0, 0))

    single = pl.Buffered(1)
    w_specs = [w_spec(s, t) for t in range(2)
               for s in ((D_MODEL, EXPERT_FF), (D_MODEL, EXPERT_FF), (EXPERT_FF, D_MODEL))]
    grid_spec = pltpu.PrefetchScalarGridSpec(
        num_scalar_prefetch=2,
        grid=(n_groups, (max_blocks + 1) // 2 + 1),
        in_specs=[
            pl.BlockSpec((1, 2, gsz), lambda g, b, be, nv: (g, 0, 0), memory_space=pltpu.SMEM),
            pl.BlockSpec((gsz, HALF), lambda g, b, be, nv: (g, 0), pipeline_mode=single),
        ] + w_specs,
        out_specs=pl.BlockSpec((1, 2 * gsz + SUBLANES, HALF), lambda g, b, be, nv: (g, 0, 0),
                               pipeline_mode=single),
        scratch_shapes=[pltpu.VMEM((4, MOE_BLOCK, HALF), jnp.uint32), pltpu.VMEM((4, MOE_BLOCK, HALF), jnp.uint32),
                        pltpu.SMEM((max_blocks * MOE_BLOCK,), jnp.int32)],
    )
    ws = (w["w_e_gate"], w["w_e_up"], w["w_e_down"])
    return pl.pallas_call(
        functools.partial(_expert_kernel, gsz=gsz, max_blocks=max_blocks), grid_spec=grid_spec,
        out_shape=jax.ShapeDtypeStruct((n_groups, 2 * gsz + SUBLANES, HALF), jnp.uint32),
        compiler_params=_params(),
    )(block_e, n_valid, dest, xpk, *ws, *ws)


def _combine_kernel(x_ref, mod_ref, y0_ref, y1_ref, rt_ref, lg_ref, lb_ref, o_ref, *, bb, lb):
    rt = rt_ref[...]
    w0 = rt[:, 2:3]
    w1 = rt[:, 3:4]
    h0, l0 = _unpack_bf16_pair(y0_ref[0])
    h1, l1 = _unpack_bf16_pair(y1_ref[0])
    f = jnp.concatenate([w0 * h0 + w1 * h1, w0 * l0 + w1 * l1], axis=1).reshape(bb, lb, D_MODEL)
    t = DN_ALPHA * x_ref[...] + mod_ref[:, 5:6, :] * f
    o_ref[...] = _layer_norm(t, lg_ref[...], lb_ref[...])


def _combine_call(x1, ada, ys, route, w, l, *, bb, lb, gsz, mod_off):
    b, seq, _ = x1.shape
    r = bb * lb
    nl = seq // lb
    tiles_per_group = gsz // r

    def y_map(k):
        def m(i, j):
            t = i * nl + j
            return (t // tiles_per_group, k * tiles_per_group + t % tiles_per_group, 0)
        return m

    return pl.pallas_call(
        functools.partial(_combine_kernel, bb=bb, lb=lb),
        grid=(b // bb, nl),
        in_specs=[pl.BlockSpec((bb, lb, D_MODEL), lambda i, j: (i, j, 0)),
                  pl.BlockSpec((None, bb, 6, D_MODEL), lambda i, j: (l, mod_off // bb + i, 0, 0)),
                  pl.BlockSpec((1, r, HALF), y_map(0)),
                  pl.BlockSpec((1, r, HALF), y_map(1)),
                  pl.BlockSpec((r, LANES), lambda i, j: (i * nl + j, 0)),
                  _layer_spec(w["ln2_g"], l), _layer_spec(w["ln2_b"], l)],
        out_specs=pl.BlockSpec((bb, lb, D_MODEL), lambda i, j: (i, j, 0)),
        out_shape=jax.ShapeDtypeStruct((b, seq, D_MODEL), f32),
        compiler_params=_params(),
    )(x1, ada, ys, ys, route, w["ln2_g"], w["ln2_b"])


def _moe(x1, ada, xpk, route, w, l, *, bb, lb, gsz, mod_off):
    n = xpk.shape[0]
    n_groups = n // gsz
    max_blocks = (2 * gsz + N_EXPERTS * (MOE_BLOCK - 1)) // MOE_BLOCK
    dest, counts = _rank_call(route, n_groups=n_groups, gsz=gsz)
    cnt = counts[:, 0, :N_EXPERTS].astype(jnp.int32)
    end_blk = jnp.cumsum((cnt + MOE_BLOCK - 1) // MOE_BLOCK, axis=-1)
    n_valid = end_blk[:, -1]
    blk = jnp.arange(max_blocks, dtype=jnp.int32)
    block_e = jnp.sum(blk[None, :, None] >= end_blk[:, None, :], axis=-1).astype(jnp.int32)
    block_e = jnp.minimum(block_e, N_EXPERTS - 1).reshape(n_groups * max_blocks)
    ys = _expert_call(xpk, block_e, n_valid, dest, w, l, n_groups=n_groups, gsz=gsz, max_blocks=max_blocks)
    return _combine_call(x1, ada, ys, route, w, l, bb=bb, lb=lb, gsz=gsz, mod_off=mod_off)


def _win_kernel(w_ref, wa_ref, wgl_ref):
    wa_ref[0] = w_ref[0, :, 0:WA_WIDTH].astype(bf16)
    wgl_ref[0] = w_ref[0, :, GATE_OFF:GATE_OFF + N_BRANCH * D_MODEL].astype(bf16)


def _win_call(w_in):
    rows = 128
    width = w_in.shape[-1]
    return pl.pallas_call(
        _win_kernel, grid=(DEPTH, D_MODEL // rows),
        in_specs=[pl.BlockSpec((1, rows, width), lambda l, i: (l, i, 0))],
        out_specs=(pl.BlockSpec((1, rows, WA_WIDTH), lambda l, i: (l, i, 0)),
                   pl.BlockSpec((1, rows, N_BRANCH * D_MODEL), lambda l, i: (l, i, 0))),
        out_shape=(jax.ShapeDtypeStruct((DEPTH, D_MODEL, WA_WIDTH), bf16),
                   jax.ShapeDtypeStruct((DEPTH, D_MODEL, N_BRANCH * D_MODEL), bf16)),
        compiler_params=_params(),
    )(w_in)


def _prep_weights(p):
    wa, w_gl = _win_call(p["w_in"])
    lane_pad = lambda v: jnp.pad(v, ((0, 0), (0, LANES - v.shape[1])))[:, None, :]
    row = lambda v: v[:, None, :]
    wr = jnp.pad(jnp.concatenate([p["router_g"], p["router_e"]], axis=-1),
                 ((0, 0), (0, 0), (0, LANES - 4 - N_EXPERTS)))
    wr_hi = wr.astype(bf16)
    wr_lo = (wr - wr_hi.astype(f32)).astype(bf16)
    return dict(
        wa=wa, w_gl=w_gl,
        sgu_ln_g=row(p["sgu_ln_g"]), sgu_ln_b=row(p["sgu_ln_b"]),
        conv_w=p["conv_w"], conv_bias=row(p["conv_bias"]),
        conv_ln_g=row(p["conv_ln_g"]), conv_ln_b=row(p["conv_ln_b"]),
        pool_w=p["pool_w"], pool_scale=row(p["pool_scale"]),
        ssm_conv_w=p["ssm_conv_w"], ssm_conv_b=row(p["ssm_conv_b"]),
        dt_bias=lane_pad(p["ssm_dt_bias"]), a_log=lane_pad(p["ssm_a_log"]),
        d_skip=row(jnp.repeat(p["ssm_d"], SSM_HEADDIM, axis=-1)), norm_g=row(p["ssm_norm_g"]),
        w_br=jnp.concatenate([p["w_br_a"], p["w_br_b"], p["w_br_c"], p["w_br_d"]], axis=1).astype(bf16),
        w_o=p["w_o"].astype(bf16),
        ln1_g=row(p["ln1_g"]), ln1_b=row(p["ln1_b"]),
        w_router=jnp.stack([wr_hi, wr_lo], axis=1),
        w_e_gate=p["w_e_gate"].astype(bf16), w_e_up=p["w_e_up"].astype(bf16), w_e_down=p["w_e_down"].astype(bf16),
        ln2_g=row(p["ln2_g"]), ln2_b=row(p["ln2_b"]),
    )


def _prep_sgu(p, q, r):
    nb = r // q
    idx = np.arange(r)
    keep = (idx[:, None] // q == idx[None, :] // q) & (idx[None, :] % q <= idx[:, None] % q)
    sgu_w = jnp.tile(p["sgu_w"][:, :, :q, :q], (1, 1, nb, nb))
    sgu_w = jnp.where(keep, sgu_w, 0.0).astype(bf16)
    bias = jnp.tile(p["sgu_b"][:, :, :q], (1, 1, nb))
    bias = jnp.repeat(jnp.swapaxes(bias, 1, 2), LANES, axis=-1)
    return dict(sgu_w=sgu_w, sgu_bias_rows=bias)


PROMPT_MIX = (1, 128)
SAMPLE_MIX = (8, 8)
PROMPT_TOK = (1, 256)
SAMPLE_TOK = (32, 8)
PROMPT_GROUP = 4096
SAMPLE_GROUP = 1024


def kernel(x_prompt, x_sample, state_conv, state_pool, state_ssm_conv, state_ssm, c_prompt, c_sample,
           w_ada, b_ada, w_in, sgu_ln_g, sgu_ln_b, sgu_w, sgu_b, conv_w, conv_bias, conv_ln_g, conv_ln_b,
           pool_w, pool_scale, ssm_conv_w, ssm_conv_b, ssm_dt_bias, ssm_a_log, ssm_d, ssm_norm_g,
           w_br_a, w_br_b, w_br_c, w_br_d, w_o, ln1_g, ln1_b, router_g, router_e, w_e_gate, w_e_up,
           w_e_down, ln2_g, ln2_b):
    return _forward(x_prompt, x_sample, state_conv, state_pool, state_ssm_conv, state_ssm, c_prompt, c_sample,
                    w_ada, b_ada, w_in, sgu_ln_g, sgu_ln_b, sgu_w, sgu_b, conv_w, conv_bias, conv_ln_g, conv_ln_b,
                    pool_w, pool_scale, ssm_conv_w, ssm_conv_b, ssm_dt_bias, ssm_a_log, ssm_d, ssm_norm_g,
                    w_br_a, w_br_b, w_br_c, w_br_d, w_o, ln1_g, ln1_b, router_g, router_e, w_e_gate, w_e_up,
                    w_e_down, ln2_g, ln2_b)


def _forward(x_prompt, x_sample, state_conv, state_pool, state_ssm_conv, state_ssm, c_prompt, c_sample,
             w_ada, b_ada, w_in, sgu_ln_g, sgu_ln_b, sgu_w, sgu_b, conv_w, conv_bias, conv_ln_g, conv_ln_b,
             pool_w, pool_scale, ssm_conv_w, ssm_conv_b, ssm_dt_bias, ssm_a_log, ssm_d, ssm_norm_g,
             w_br_a, w_br_b, w_br_c, w_br_d, w_o, ln1_g, ln1_b, router_g, router_e, w_e_gate, w_e_up,
             w_e_down, ln2_g, ln2_b, prompt_group=PROMPT_GROUP, sample_group=SAMPLE_GROUP,
             sample_tok=SAMPLE_TOK):
    p = dict(w_in=w_in, sgu_ln_g=sgu_ln_g, sgu_ln_b=sgu_ln_b, sgu_w=sgu_w, sgu_b=sgu_b, conv_w=conv_w,
             conv_bias=conv_bias, conv_ln_g=conv_ln_g, conv_ln_b=conv_ln_b, pool_w=pool_w, pool_scale=pool_scale,
             ssm_conv_w=ssm_conv_w, ssm_conv_b=ssm_conv_b, ssm_dt_bias=ssm_dt_bias, ssm_a_log=ssm_a_log,
             ssm_d=ssm_d, ssm_norm_g=ssm_norm_g, w_br_a=w_br_a, w_br_b=w_br_b, w_br_c=w_br_c, w_br_d=w_br_d,
             w_o=w_o, ln1_g=ln1_g, ln1_b=ln1_b, router_g=router_g, router_e=router_e, ln2_g=ln2_g, ln2_b=ln2_b,
             w_e_gate=w_e_gate, w_e_up=w_e_up, w_e_down=w_e_down)
    bp = x_prompt.shape[0]
    bs = x_sample.shape[0]
    w = _prep_weights(p)
    ada = _ada_call(jnp.concatenate([c_sample, c_prompt], axis=0), w_ada, b_ada)
    ada = ada.reshape(DEPTH, bs + bp, 6, D_MODEL)

    groups = [
        dict(x=x_prompt, mix=PROMPT_MIX, tok=PROMPT_TOK, gsz=prompt_group, start=0, mod_off=bs, vn=None,
             states=tuple(jnp.zeros((DEPTH, bp) + s, f32) for s in STATE_SHAPES)),
        dict(x=x_sample, mix=SAMPLE_MIX, tok=sample_tok, gsz=sample_group, start=PAST_LEN, mod_off=0,
             vn=jnp.zeros((DEPTH, bs, x_sample.shape[1], A_WIDTH), f32),
             states=(state_conv, state_pool, state_ssm_conv, state_ssm.reshape(DEPTH, bs, D_WIDTH, SSM_STATE))),
    ]
    for g in groups:
        g["w"] = dict(w, **_prep_sgu(p, g["mix"][1], g["mix"][0] * g["mix"][1]))
    for l in range(DEPTH):
        for g in groups:
            bb, lb = g["mix"]
            ycat, g["states"], g["vn"] = _mixer_call(
                g["x"], ada, g["states"], g["vn"], g["w"], l, bb=bb, lb=lb, start_pos=g["start"],
                mod_off=g["mod_off"])
            bb, lb = g["tok"]
            x1, xpk, route = _merge_call(g["x"], ada, ycat, g["w"], l, bb=bb, lb=lb, mod_off=g["mod_off"])
            g["x"] = _moe(x1, ada, xpk, route, g["w"], l, bb=bb, lb=lb, gsz=g["gsz"], mod_off=g["mod_off"])

    def states_out(g, b):
        c, pl_, sc, sm = g["states"]
        return c, pl_, sc, sm.reshape(DEPTH, b, SSM_HEADS, SSM_HEADDIM, SSM_STATE)

    return ((groups[0]["x"], groups[1]["x"]) + states_out(groups[0], bp) + states_out(groups[1], bs)
            + (groups[1]["vn"],))
```

```python
import functools
import math

import jax
import jax.numpy as jnp
import numpy as np
from jax import lax
from jax.experimental import pallas as pl
from jax.experimental.pallas import tpu as pltpu

D_MODEL = 1024
DEPTH = 4
A_WIDTH = 512
A_GROUPS = 4
B_WIDTH = 512
CONV_W = 31
C_WIDTH = 512
POOL_WINDOWS = (2, 4, 8, 16)
POOL_BUF = 15
SSM_HEADS = 12
SSM_HEADDIM = 64
D_WIDTH = SSM_HEADS * SSM_HEADDIM
SSM_GROUPS = 4
SSM_STATE = 128
SSM_CONV = 4
XBC_WIDTH = D_WIDTH + 2 * SSM_GROUPS * SSM_STATE
N_BRANCH = 4
E_PER_GROUP = 8
N_EXPERTS = 32
EXPERT_FF = 512
MOE_BLOCK = 128
DN_ALPHA = (2 * DEPTH) ** 0.25
LN_EPS = 1e-5
PAST_LEN = 16384
Y_WIDTH = A_WIDTH + B_WIDTH + C_WIDTH + D_WIDTH
HALF = D_MODEL // 2

OFF_U, OFF_V, OFF_GA, OFF_GG, OFF_P, OFF_Z, OFF_XBC, OFF_DT = 0, 512, 1024, 1536, 2048, 2560, 3328, 5120
GATE_OFF = OFF_DT + SSM_HEADS
WA_WIDTH = 5248
LANES = 128
SUBLANES = 8
HIST_CONV = 32
HIST_POOL = 16
HIST_SCONV = 8
VMEM_LIMIT = 56 * 1024 * 1024
RANK_TILE = 512

f32 = jnp.float32
bf16 = jnp.bfloat16


def _dot(a, b):
    return jnp.dot(a, b, preferred_element_type=f32)


def _dot_nt(a, b):
    return lax.dot_general(a, b, (((1,), (1,)), ((), ())), preferred_element_type=f32)


def _dot_tn(a, b):
    return lax.dot_general(a, b, (((0,), (0,)), ((), ())), preferred_element_type=f32)


def _split3(x):
    h1 = x.astype(bf16)
    r1 = x - h1.astype(f32)
    h2 = r1.astype(bf16)
    r2 = r1 - h2.astype(f32)
    return h1, h2, r2.astype(bf16)


def _sigmoid(x):
    return 0.5 * (jnp.tanh(0.5 * x) + 1.0)


def _silu(x):
    return x * _sigmoid(x)


def _gelu(x):
    return 0.5 * x * (1.0 + jnp.tanh(math.sqrt(2.0 / math.pi) * (x + 0.044715 * (x * x * x))))


def _softplus(x):
    return jnp.maximum(x, 0.0) + jnp.log1p(jnp.exp(-jnp.abs(x)))


def _layer_norm(x, g, b):
    mu = jnp.mean(x, axis=-1, keepdims=True)
    xc = x - mu
    var = jnp.mean(xc * xc, axis=-1, keepdims=True)
    return xc * lax.rsqrt(var + LN_EPS) * g + b


def _layer_spec(a, l):
    nd = a.ndim
    return pl.BlockSpec((None,) + a.shape[1:], lambda *_: (l,) + (0,) * (nd - 1))


def _params(n_axes=2):
    return pltpu.CompilerParams(dimension_semantics=("arbitrary",) * n_axes, vmem_limit_bytes=VMEM_LIMIT)


def _ada_kernel(c_ref, w_ref, b_ref, o_ref):
    a = _silu(c_ref[...]).astype(bf16)
    o_ref[0] = _dot(a, w_ref[0].astype(bf16)) + b_ref[0]


def _ada_call(c_all, w_ada, b_ada):
    n = c_all.shape[0]
    return pl.pallas_call(
        _ada_kernel,
        grid=(DEPTH, 6),
        in_specs=[
            pl.BlockSpec((n, D_MODEL), lambda l, j: (0, 0)),
            pl.BlockSpec((1, D_MODEL, D_MODEL), lambda l, j: (l, 0, j)),
            pl.BlockSpec((1, 1, D_MODEL), lambda l, j: (l, 0, j)),
        ],
        out_specs=pl.BlockSpec((1, n, D_MODEL), lambda l, j: (l, 0, j)),
        out_shape=jax.ShapeDtypeStruct((DEPTH, n, 6 * D_MODEL), f32),
        compiler_params=_params(),
    )(c_all, w_ada, b_ada.reshape(DEPTH, 1, 6 * D_MODEL))


def _mixer_kernel(x_ref, mod_ref, stc_ref, stp_ref, sts_ref, stm_ref, *rest, bb, lb, start_pos, emit_vn):
    if emit_vn:
        rest = rest[1:]
    (wa_ref, sgw_ref, sgb_ref, slg_ref, slb_ref, cw_ref, cb_ref, clg_ref, clb_ref, pw_ref, ps_ref,
     scw_ref, scb_ref, dtb_ref, alog_ref, dsk_ref, ng_ref, y_ref, nc_ref, np_ref, ns_ref, nm_ref) = rest[:22]
    vn_ref = rest[22] if emit_vn else None
    hbuf, hsh, pbuf, xbuf, sst, cbuf = rest[-6:]
    r = bb * lb
    nb, q = bb, lb
    lq = q.bit_length() - 1
    j = pl.program_id(1)

    @pl.when(j == 0)
    def _():
        hbuf[:, HIST_CONV - (CONV_W - 1):HIST_CONV, :] = stc_ref[...]
        pbuf[:, HIST_POOL - POOL_BUF:HIST_POOL, :] = stp_ref[...]
        xbuf[:, HIST_SCONV - (SSM_CONV - 1):HIST_SCONV, :] = sts_ref[...]
        sst[...] = stm_ref[...]

    x = x_ref[...]
    xm = (x * (1.0 + mod_ref[:, 1:2, :]) + mod_ref[:, 0:1, :]).reshape(r, D_MODEL).astype(bf16)

    row_i = lax.broadcasted_iota(jnp.int32, (r, r), 0)
    col_i = lax.broadcasted_iota(jnp.int32, (r, r), 1)
    same = (row_i >> lq) == (col_i >> lq)
    causal = same & ((col_i & (q - 1)) <= (row_i & (q - 1)))
    causal_t = same & ((row_i & (q - 1)) <= (col_i & (q - 1)))

    ga = _dot(xm, wa_ref[:, OFF_GA:OFF_GA + B_WIDTH])
    gg = _dot(xm, wa_ref[:, OFF_GG:OFF_GG + B_WIDTH])
    h = ga * _sigmoid(gg)
    hbuf[:, HIST_CONV:HIST_CONV + lb, :] = h.reshape(bb, lb, B_WIDTH)
    base = HIST_CONV - (CONV_W - 1)
    span = HIST_CONV + lb - SUBLANES
    for s in range(1, SUBLANES):
        hsh[s - 1, :, 0:span, :] = hbuf[:, s:s + span, :]
    lc = min(lb, 64)
    for rc in range(lb // lc):
        acc = jnp.zeros((bb, lc, B_WIDTH), f32) + cb_ref[...]
        for k in range(CONV_W):
            o = base + k + rc * lc
            s, a = o % SUBLANES, o - o % SUBLANES
            tap = hbuf[:, a:a + lc, :] if s == 0 else hsh[s - 1, :, a:a + lc, :]
            acc = acc + tap * cw_ref[k:k + 1, :]
        cbuf[:, rc * lc:(rc + 1) * lc, :] = acc

    z = _dot(xm, wa_ref[:, OFF_Z:OFF_Z + D_WIDTH])
    xbc = _dot(xm, wa_ref[:, OFF_XBC:OFF_XBC + XBC_WIDTH])
    dtr = _dot(xm, wa_ref[:, OFF_DT:OFF_DT + LANES])
    xbuf[:, HIST_SCONV:HIST_SCONV + lb, :] = xbc.reshape(bb, lb, XBC_WIDTH)
    sbase = HIST_SCONV - (SSM_CONV - 1)
    acc = jnp.zeros((bb, lb, XBC_WIDTH), f32) + scb_ref[...]
    for k in range(SSM_CONV):
        acc = acc + xbuf[:, sbase + k:sbase + k + lb, :] * scw_ref[k:k + 1, :]
    xc = _silu(acc).reshape(r, XBC_WIDTH)
    new_s = xbuf[:, lb + sbase:lb + HIST_SCONV, :]
    ns_ref[...] = new_s
    xbuf[:, sbase:HIST_SCONV, :] = new_s

    u = _gelu(_dot(xm, wa_ref[:, OFF_U:OFF_U + A_WIDTH]))
    v = _gelu(_dot(xm, wa_ref[:, OFF_V:OFF_V + A_WIDTH]))
    pin = _dot(xm, wa_ref[:, OFF_P:OFF_P + C_WIDTH])
    vn = _layer_norm(v, slg_ref[...], slb_ref[...])
    if emit_vn:
        vn_ref[...] = vn.reshape(bb, lb, A_WIDTH)
    vnb = vn.astype(bf16)
    for g in range(A_GROUPS):
        sl = slice(g * LANES, (g + 1) * LANES)
        s = _dot(sgw_ref[g], vnb[:, sl]) + sgb_ref[:, sl]
        y_ref[:, sl] = (u[:, sl] * s).astype(bf16)

    yb = _silu(_layer_norm(cbuf[...].reshape(r, B_WIDTH), clg_ref[...], clb_ref[...]))
    y_ref[:, A_WIDTH:A_WIDTH + B_WIDTH] = yb.astype(bf16)
    new_c = hbuf[:, lb + base:lb + HIST_CONV, :]
    nc_ref[...] = new_c
    hbuf[:, base:HIST_CONV, :] = new_c

    pbuf[:, HIST_POOL:HIST_POOL + lb, :] = pin.reshape(bb, lb, C_WIDTH)
    pos = (start_pos + j * lb + lax.broadcasted_iota(jnp.int32, (bb, lb, LANES), 1)).astype(f32)
    for gi, win in enumerate(POOL_WINDOWS):
        sl = slice(gi * LANES, (gi + 1) * LANES)
        acc = pbuf[:, HIST_POOL:HIST_POOL + lb, sl]
        for i in range(1, win):
            acc = acc + pbuf[:, HIST_POOL - i:HIST_POOL - i + lb, sl]
        cnt = jnp.minimum(float(win), pos + 1.0)
        mixed = (acc / cnt - pbuf[:, HIST_POOL:HIST_POOL + lb, sl]).reshape(r, LANES)
        out = _dot(mixed.astype(bf16), pw_ref[gi].astype(bf16)) * ps_ref[:, sl]
        y_ref[:, A_WIDTH + B_WIDTH + gi * LANES:A_WIDTH + B_WIDTH + (gi + 1) * LANES] = out.astype(bf16)
    new_p = pbuf[:, lb + HIST_POOL - POOL_BUF:lb + HIST_POOL, :]
    np_ref[...] = new_p
    pbuf[:, HIST_POOL - POOL_BUF:HIST_POOL, :] = new_p

    dt = _softplus(dtr + dtb_ref[...])
    da = dt * (-jnp.exp(alog_ref[...]))
    d1, d2, d3 = _split3(da)
    mc = jnp.where(causal, 1.0, 0.0).astype(bf16)
    mct = jnp.where(causal_t, 1.0, 0.0).astype(bf16)
    ms = jnp.where(same, 1.0, 0.0).astype(bf16)
    acs = _dot(mc, d1) + _dot(mc, d2) + _dot(mc, d3)
    acs_t = _dot_tn(d1, mct) + _dot_tn(d2, mct) + _dot_tn(d3, mct)
    tot = _dot(ms, d1) + _dot(ms, d2) + _dot(ms, d3)

    lane = lax.broadcasted_iota(jnp.int32, (r, LANES), 1)
    lo_half = lane < SSM_HEADDIM
    cbs = []
    for g in range(SSM_GROUPS):
        bg = xc[:, D_WIDTH + g * SSM_STATE:D_WIDTH + (g + 1) * SSM_STATE].astype(bf16)
        cg = xc[:, D_WIDTH + (SSM_GROUPS + g) * SSM_STATE:D_WIDTH + (SSM_GROUPS + g + 1) * SSM_STATE].astype(bf16)
        cbs.append((bg, cg, _dot_nt(cg, bg)))
    rep = SSM_HEADS // SSM_GROUPS
    rowseq = lax.broadcasted_iota(jnp.int32, (r, LANES), 0) >> lq
    ysq = jnp.zeros((r, 1), f32)
    ypairs = []
    for i in range(SSM_HEADS // 2):
        h0, h1 = 2 * i, 2 * i + 1
        g0, g1 = h0 // rep, h1 // rep
        sl = slice(i * LANES, (i + 1) * LANES)
        xs = xc[:, sl]
        dtp = jnp.where(lo_half, dt[:, h0:h0 + 1], dt[:, h1:h1 + 1])
        acp = jnp.where(lo_half, acs[:, h0:h0 + 1], acs[:, h1:h1 + 1])
        totp = jnp.where(lo_half, tot[:, h0:h0 + 1], tot[:, h1:h1 + 1])
        xd = xs * dtp
        xdb = xd.astype(bf16)
        res = []
        for hh, gg_ in ((h0, g0), (h1, g1)):
            diff = acs[:, hh:hh + 1] - acs_t[hh:hh + 1, :]
            dec = jnp.exp(jnp.where(causal, diff, -1e30))
            res.append(_dot((cbs[gg_][2] * dec).astype(bf16), xdb))
        ydiag = jnp.where(lo_half, res[0], res[1])
        s_all = sst[:, i * LANES:(i + 1) * LANES, :].reshape(nb * LANES, SSM_STATE)
        s_bf = s_all.astype(bf16)
        full0 = _dot_nt(cbs[g0][1], s_bf)
        full1 = full0 if g1 == g0 else _dot_nt(cbs[g1][1], s_bf)
        yoff = jnp.zeros((r, LANES), f32)
        for b in range(nb):
            blk = jnp.where(lo_half, full0[:, b * LANES:(b + 1) * LANES], full1[:, b * LANES:(b + 1) * LANES])
            yoff = yoff + (jnp.where(rowseq == b, blk, 0.0) if nb > 1 else blk)
        yp = ydiag + yoff * jnp.exp(acp) + dsk_ref[:, sl] * xs
        yp = yp * _silu(z[:, sl])
        ysq = ysq + jnp.sum(yp * yp, axis=-1, keepdims=True)
        ypairs.append(yp)
        xdd = xd * jnp.exp(totp - acp)
        if nb > 1:
            colblk = lax.broadcasted_iota(jnp.int32, (r, nb * LANES), 1) >> 7
            rowblk = lax.broadcasted_iota(jnp.int32, (r, nb * LANES), 0) >> lq
            xblk = jnp.where(colblk == rowblk, jnp.concatenate([xdd] * nb, axis=1), 0.0).astype(bf16)
        else:
            xblk = xdd.astype(bf16)
        upd0 = _dot_tn(xblk, cbs[g0][0])
        if g1 != g0:
            upd1 = _dot_tn(xblk, cbs[g1][0])
            prow = lax.broadcasted_iota(jnp.int32, (nb * LANES, SSM_STATE), 0) & (LANES - 1)
            upd0 = jnp.where(prow < SSM_HEADDIM, upd0, upd1)
        upd0 = upd0.reshape(nb, LANES, SSM_STATE)
        for b in range(nb):
            for hh, half in ((h0, 0), (h1, 1)):
                fac = jnp.exp(jnp.broadcast_to(tot[b * q:b * q + 1, hh:hh + 1], (SSM_HEADDIM, SSM_STATE)))
                rs = slice(i * LANES + half * SSM_HEADDIM, i * LANES + (half + 1) * SSM_HEADDIM)
                sst[b, rs, :] = fac * sst[b, rs, :] + upd0[b, half * SSM_HEADDIM:(half + 1) * SSM_HEADDIM, :]
    rms = lax.rsqrt(ysq * (1.0 / D_WIDTH) + LN_EPS)
    for i in range(SSM_HEADS // 2):
        sl = slice(i * LANES, (i + 1) * LANES)
        yd = ypairs[i] * rms * ng_ref[:, sl]
        o = A_WIDTH + B_WIDTH + C_WIDTH + i * LANES
        y_ref[:, o:o + LANES] = yd.astype(bf16)
    nm_ref[...] = sst[...]


def _merge_kernel(x_ref, mod_ref, y_ref, wg_ref, wbr_ref, wo_ref, lg_ref, lb_ref, wr_ref,
                  x1_ref, xpk_ref, rt_ref, *, bb, lb):
    r = bb * lb
    x = x_ref[...]
    xm = (x * (1.0 + mod_ref[:, 1:2, :]) + mod_ref[:, 0:1, :]).reshape(r, D_MODEL).astype(bf16)
    y = y_ref[...]
    offs = (0, A_WIDTH, A_WIDTH + B_WIDTH, A_WIDTH + B_WIDTH + C_WIDTH, Y_WIDTH)
    merged = jnp.zeros((r, D_MODEL), f32)
    for k in range(N_BRANCH):
        gate = _sigmoid(_dot(xm, wg_ref[:, k * D_MODEL:(k + 1) * D_MODEL]))
        merged = merged + gate * _dot(y[:, offs[k]:offs[k + 1]], wbr_ref[offs[k]:offs[k + 1], :])
    mix = _dot(merged.astype(bf16), wo_ref[...]).reshape(bb, lb, D_MODEL)
    x1 = _layer_norm(DN_ALPHA * x + mod_ref[:, 2:3, :] * mix, lg_ref[...], lb_ref[...])
    x1_ref[...] = x1
    xm2 = (x1 * (1.0 + mod_ref[:, 4:5, :]) + mod_ref[:, 3:4, :]).reshape(r, D_MODEL)
    xpk_ref[...] = _pack_bf16_pair(xm2[:, :HALF], xm2[:, HALF:])
    rt_ref[...] = _route(xm2, wr_ref)


def _route(xm2, wr_ref):
    r = xm2.shape[0]
    a1, a2, _ = _split3(xm2)
    lg = _dot(a1, wr_ref[0]) + (_dot(a2, wr_ref[0]) + _dot(a1, wr_ref[1]))
    lane = lax.broadcasted_iota(jnp.int32, (r, LANES), 1).astype(f32)
    neg, big = -1e30, 1e6
    mask_g = lane < 4.0
    lgg = jnp.where(mask_g, lg, neg)
    mg = jnp.max(lgg, axis=-1, keepdims=True)
    grp = jnp.min(jnp.where(lgg == mg, lane, big), axis=-1, keepdims=True)
    p_grp = 1.0 / jnp.sum(jnp.where(mask_g, jnp.exp(lgg - mg), 0.0), axis=-1, keepdims=True)
    lo = 4.0 + grp * E_PER_GROUP
    mask_e = (lane >= lo) & (lane < lo + E_PER_GROUP)
    lge = jnp.where(mask_e, lg, neg)
    v1 = jnp.max(lge, axis=-1, keepdims=True)
    i1 = jnp.min(jnp.where(lge == v1, lane, big), axis=-1, keepdims=True)
    lge2 = jnp.where(lane == i1, neg, lge)
    v2 = jnp.max(lge2, axis=-1, keepdims=True)
    i2 = jnp.min(jnp.where(lge2 == v2, lane, big), axis=-1, keepdims=True)
    e21 = jnp.exp(v2 - v1)
    w1 = p_grp / (1.0 + e21)
    w2 = p_grp * e21 / (1.0 + e21)
    return jnp.where(lane == 0.0, i1 - 4.0,
                     jnp.where(lane == 1.0, i2 - 4.0,
                               jnp.where(lane == 2.0, w1, jnp.where(lane == 3.0, w2, 0.0))))


MIXER_WEIGHTS = ("wa", "sgu_w", "sgu_bias_rows", "sgu_ln_g", "sgu_ln_b", "conv_w", "conv_bias", "conv_ln_g",
                 "conv_ln_b", "pool_w", "pool_scale", "ssm_conv_w", "ssm_conv_b", "dt_bias", "a_log", "d_skip",
                 "norm_g")
MERGE_WEIGHTS = ("w_gl", "w_br", "w_o", "ln1_g", "ln1_b", "w_router")
BIG_WEIGHTS = ("wa", "w_gl", "w_br", "w_o")
STATE_SHAPES = ((CONV_W - 1, B_WIDTH), (POOL_BUF, C_WIDTH), (SSM_CONV - 1, XBC_WIDTH), (D_WIDTH, SSM_STATE))


def _weight_spec(w, name, l):
    spec = _layer_spec(w[name], l)
    if name in BIG_WEIGHTS:
        spec = pl.BlockSpec(spec.block_shape, spec.index_map, pipeline_mode=pl.Buffered(1))
    return spec


def _mixer_call(x, ada, states, vn_buf, w, l, *, bb, lb, start_pos, mod_off):
    b, seq, _ = x.shape
    r = bb * lb
    nl = seq // lb
    emit_vn = vn_buf is not None
    kern = functools.partial(_mixer_kernel, bb=bb, lb=lb, start_pos=start_pos, emit_vn=emit_vn)

    def state_spec(shape):
        return pl.BlockSpec((None, bb) + shape, lambda i, j: (l, i) + (0,) * len(shape))

    weights = [w[n] for n in MIXER_WEIGHTS]
    extra_in = [vn_buf] if emit_vn else []
    in_specs = ([pl.BlockSpec((bb, lb, D_MODEL), lambda i, j: (i, j, 0)),
                 pl.BlockSpec((None, bb, 6, D_MODEL), lambda i, j: (l, mod_off // bb + i, 0, 0))]
                + [state_spec(s) for s in STATE_SHAPES]
                + [pl.BlockSpec(memory_space=pl.ANY) for _ in extra_in]
                + [_weight_spec(w, n, l) for n in MIXER_WEIGHTS])
    n_st = len(STATE_SHAPES)
    out_shape = [jax.ShapeDtypeStruct((b * seq, Y_WIDTH), bf16)]
    out_shape += [jax.ShapeDtypeStruct(s.shape, f32) for s in states]
    out_specs = [pl.BlockSpec((r, Y_WIDTH), lambda i, j: (i * nl + j, 0))] + [state_spec(s) for s in STATE_SHAPES]
    aliases = {2 + k: 1 + k for k in range(n_st)}
    if emit_vn:
        out_shape.append(jax.ShapeDtypeStruct(vn_buf.shape, f32))
        out_specs.append(pl.BlockSpec((None, bb, lb, A_WIDTH), lambda i, j: (l, i, j, 0)))
        aliases[2 + n_st] = 1 + n_st
    scratch = [
        pltpu.VMEM((bb, HIST_CONV + lb, B_WIDTH), f32),
        pltpu.VMEM((SUBLANES - 1, bb, HIST_CONV + lb, B_WIDTH), f32),
        pltpu.VMEM((bb, HIST_POOL + lb, C_WIDTH), f32),
        pltpu.VMEM((bb, HIST_SCONV + lb, XBC_WIDTH), f32),
        pltpu.VMEM((bb, D_WIDTH, SSM_STATE), f32),
        pltpu.VMEM((bb, lb, B_WIDTH), f32),
    ]
    res = pl.pallas_call(
        kern, grid=(b // bb, nl), in_specs=in_specs, out_specs=out_specs, out_shape=out_shape,
        scratch_shapes=scratch, input_output_aliases=aliases, compiler_params=_params(),
    )(x, ada, *states, *extra_in, *weights)
    return res[0], tuple(res[1:1 + n_st]), (res[1 + n_st] if emit_vn else None)


def _merge_call(x, ada, ycat, w, l, *, bb, lb, mod_off):
    b, seq, _ = x.shape
    r = bb * lb
    nl = seq // lb
    n = b * seq
    weights = [w[k] for k in MERGE_WEIGHTS]
    return pl.pallas_call(
        functools.partial(_merge_kernel, bb=bb, lb=lb), grid=(b // bb, nl),
        in_specs=[pl.BlockSpec((bb, lb, D_MODEL), lambda i, j: (i, j, 0)),
                  pl.BlockSpec((None, bb, 6, D_MODEL), lambda i, j: (l, mod_off // bb + i, 0, 0)),
                  pl.BlockSpec((r, Y_WIDTH), lambda i, j: (i * nl + j, 0))]
        + [_weight_spec(w, k, l) for k in MERGE_WEIGHTS],
        out_specs=(pl.BlockSpec((bb, lb, D_MODEL), lambda i, j: (i, j, 0)),
                   pl.BlockSpec((r, HALF), lambda i, j: (i * nl + j, 0)),
                   pl.BlockSpec((r, LANES), lambda i, j: (i * nl + j, 0))),
        out_shape=(jax.ShapeDtypeStruct((b, seq, D_MODEL), f32),
                   jax.ShapeDtypeStruct((n, HALF), jnp.uint32),
                   jax.ShapeDtypeStruct((n, LANES), f32)),
        compiler_params=_params(),
    )(x, ada, ycat, *weights)


def _pack_bf16_pair(a, b):
    ua = lax.bitcast_convert_type(a.astype(bf16).astype(f32), jnp.uint32)
    ub = lax.bitcast_convert_type(b.astype(bf16).astype(f32), jnp.uint32)
    return (ua & jnp.uint32(0xFFFF0000)) | (ub >> 16)


def _unpack_bf16_pair(u):
    hi = lax.bitcast_convert_type(u & jnp.uint32(0xFFFF0000), f32)
    lo = lax.bitcast_convert_type(u << 16, f32)
    return hi, lo


def _rank_kernel(rt_ref, dest_ref, cnt_ref, carry, start, *, t, n_tiles):
    p = pl.program_id(1)
    j = pl.program_id(2)
    rt = rt_ref[...]
    lane = lax.broadcasted_iota(jnp.int32, (t, LANES), 1).astype(f32)
    oh0 = jnp.where(lane == rt[:, 0:1], 1.0, 0.0)
    oh1 = jnp.where(lane == rt[:, 1:2], 1.0, 0.0)
    both = oh0 + oh1
    colsum = jnp.sum(both, axis=0, keepdims=True)

    @pl.when((p == 0) & (j == 0))
    def _():
        carry[...] = jnp.zeros(carry.shape, f32)

    @pl.when(p == 0)
    def _():
        carry[0:1, :] = carry[0:1, :] + colsum

    @pl.when((p == 0) & (j == n_tiles - 1))
    def _():
        c = jnp.broadcast_to(carry[0:1, :], (SUBLANES, LANES))
        cnt_ref[0] = c
        nblk = jnp.floor((c + (MOE_BLOCK - 1.0)) * (1.0 / MOE_BLOCK))
        ri = lax.broadcasted_iota(jnp.int32, (LANES, LANES), 0)
        ci = lax.broadcasted_iota(jnp.int32, (LANES, LANES), 1)
        tri = jnp.where(ri < ci, 1.0, 0.0).astype(bf16)
        start[...] = _dot(nblk.astype(bf16), tri) * float(MOE_BLOCK)
        carry[...] = jnp.zeros(carry.shape, f32)

    @pl.when(p == 1)
    def _():
        rr = lax.broadcasted_iota(jnp.int32, (t, t), 0)
        cc = lax.broadcasted_iota(jnp.int32, (t, t), 1)
        ltri = jnp.where(cc < rr, 1.0, 0.0).astype(bf16)
        base = start[0:1, :] + carry[0:1, :] + _dot(ltri, both.astype(bf16))
        d0 = jnp.sum(oh0 * base, axis=-1, keepdims=True)
        d1 = jnp.sum(oh1 * base, axis=-1, keepdims=True)
        arr = jnp.where(lane == 0.0, d0, jnp.where(lane == 1.0, d1, 0.0))
        dest_ref[0] = arr.T[0:2, :].astype(jnp.int32)
        carry[0:1, :] = carry[0:1, :] + colsum


def _rank_call(route, *, n_groups, gsz):
    t = min(RANK_TILE, gsz)
    n_tiles = gsz // t
    return pl.pallas_call(
        functools.partial(_rank_kernel, t=t, n_tiles=n_tiles),
        grid=(n_groups, 2, n_tiles),
        in_specs=[pl.BlockSpec((t, LANES), lambda g, p, j: (g * n_tiles + j, 0))],
        out_specs=(pl.BlockSpec((1, 2, t), lambda g, p, j: (g, 0, j * p)),
                   pl.BlockSpec((1, SUBLANES, LANES), lambda g, p, j: (g, 0, 0))),
        out_shape=(jax.ShapeDtypeStruct((n_groups, 2, gsz), jnp.int32),
                   jax.ShapeDtypeStruct((n_groups, SUBLANES, LANES), f32)),
        scratch_shapes=[pltpu.VMEM((SUBLANES, LANES), f32), pltpu.VMEM((SUBLANES, LANES), f32)],
        compiler_params=_params(3),
    )(route)


def _expert_kernel(be_ref, nv_ref, dest_ref, x_ref, wg0_ref, wu0_ref, wd0_ref, wg1_ref, wu1_ref, wd1_ref,
                   o_ref, xg, yg, tbl, *, gsz, max_blocks):
    g = pl.program_id(0)
    b = pl.program_id(1)
    pad_row = 2 * gsz

    nv = nv_ref[g]
    xgs = tuple(xg.at[k] for k in range(4))
    ygs = tuple(yg.at[k] for k in range(4))
    w_refs = ((wg0_ref, wu0_ref, wd0_ref), (wg1_ref, wu1_ref, wd1_ref))

    @pl.when(b == 0)
    def _():
        def init(i, c):
            tbl[i] = pad_row
            return c

        lax.fori_loop(0, max_blocks * MOE_BLOCK, init, 0, unroll=8)

        def fill(t, c):
            tbl[dest_ref[0, 0, t]] = t
            tbl[dest_ref[0, 1, t]] = gsz + t
            return c

        lax.fori_loop(0, gsz, fill, 0, unroll=8)
        o_ref[0, pad_row:pad_row + SUBLANES, :] = jnp.zeros((SUBLANES, HALF), jnp.uint32)
        yg[...] = jnp.zeros(yg.shape, jnp.uint32)

        def gather0(i, c):
            xg[0, pl.ds(i, 1), :] = x_ref[pl.ds(tbl[i] & (gsz - 1), 1), :]
            second = jnp.minimum(1, nv - 1) * MOE_BLOCK
            xg[1, pl.ds(i, 1), :] = x_ref[pl.ds(tbl[second + i] & (gsz - 1), 1), :]
            return c

        lax.fori_loop(0, MOE_BLOCK, gather0, 0, unroll=8)

    def step(par):
        nxt = 1 - par
        for t in range(2):
            kn = 2 * (b + 1) + t
            kp = 2 * (b - 1) + t
            base_n = jnp.minimum(kn, nv - 1) * MOE_BLOCK
            base_p = jnp.clip(kp, 0, nv - 1) * MOE_BLOCK
            live = (kp >= 0) & (kp < nv)
            x_nxt, y_prv = xgs[2 * nxt + t], ygs[2 * nxt + t]
            for i in range(MOE_BLOCK):
                x_nxt[i:i + 1, :] = x_ref[pl.ds(tbl[base_n + i] & (gsz - 1), 1), :]
                s = jnp.where(live, tbl[base_p + i], pad_row)
                o_ref[0, pl.ds(s, 1), :] = y_prv[i:i + 1, :]
        for t in range(2):
            wg_ref, wu_ref, wd_ref = w_refs[t]
            hi, lo = _unpack_bf16_pair(xgs[2 * par + t][...])
            hi = hi.astype(bf16)
            lo = lo.astype(bf16)
            hg = _dot(hi, wg_ref[:HALF, :]) + _dot(lo, wg_ref[HALF:, :])
            hu = _dot(hi, wu_ref[:HALF, :]) + _dot(lo, wu_ref[HALF:, :])
            hid = (_silu(hg) * hu).astype(bf16)
            y = _dot(hid, wd_ref[...])
            ygs[2 * par + t][...] = _pack_bf16_pair(y[:, :HALF], y[:, HALF:])

    @pl.when((2 * b <= nv + 1) & (b % 2 == 0))
    def _():
        step(0)

    @pl.when((2 * b <= nv + 1) & (b % 2 == 1))
    def _():
        step(1)


def _expert_call(xpk, block_e, n_valid, dest, w, l, *, n_groups, gsz, max_blocks):
    assert gsz & (gsz - 1) == 0

    def w_spec(shape, t):
        return pl.BlockSpec(
            (None, None) + shape,
            lambda g, b, be, nv: (l, be[g * max_blocks + jnp.minimum(2 * b + t, max_blocks - 1)], 0, 0))

    single = pl.Buffered(1)
    w_specs = [w_spec(s, t) for t in range(2)
               for s in ((D_MODEL, EXPERT_FF), (D_MODEL, EXPERT_FF), (EXPERT_FF, D_MODEL))]
    grid_spec = pltpu.PrefetchScalarGridSpec(
        num_scalar_prefetch=2,
        grid=(n_groups, (max_blocks + 1) // 2 + 1),
        in_specs=[
            pl.BlockSpec((1, 2, gsz), lambda g, b, be, nv: (g, 0, 0), memory_space=pltpu.SMEM),
            pl.BlockSpec((gsz, HALF), lambda g, b, be, nv: (g, 0), pipeline_mode=single),
        ] + w_specs,
        out_specs=pl.BlockSpec((1, 2 * gsz + SUBLANES, HALF), lambda g, b, be, nv: (g, 0, 0),
                               pipeline_mode=single),
        scratch_shapes=[pltpu.VMEM((4, MOE_BLOCK, HALF), jnp.uint32), pltpu.VMEM((4, MOE_BLOCK, HALF), jnp.uint32),
                        pltpu.SMEM((max_blocks * MOE_BLOCK,), jnp.int32)],
    )
    ws = (w["w_e_gate"], w["w_e_up"], w["w_e_down"])
    return pl.pallas_call(
        functools.partial(_expert_kernel, gsz=gsz, max_blocks=max_blocks), grid_spec=grid_spec,
        out_shape=jax.ShapeDtypeStruct((n_groups, 2 * gsz + SUBLANES, HALF), jnp.uint32),
        compiler_params=_params(),
    )(block_e, n_valid, dest, xpk, *ws, *ws)


def _combine_kernel(x_ref, mod_ref, y0_ref, y1_ref, rt_ref, lg_ref, lb_ref, o_ref, *, bb, lb):
    rt = rt_ref[...]
    w0 = rt[:, 2:3]
    w1 = rt[:, 3:4]
    h0, l0 = _unpack_bf16_pair(y0_ref[0])
    h1, l1 = _unpack_bf16_pair(y1_ref[0])
    f = jnp.concatenate([w0 * h0 + w1 * h1, w0 * l0 + w1 * l1], axis=1).reshape(bb, lb, D_MODEL)
    t = DN_ALPHA * x_ref[...] + mod_ref[:, 5:6, :] * f
    o_ref[...] = _layer_norm(t, lg_ref[...], lb_ref[...])


def _combine_call(x1, ada, ys, route, w, l, *, bb, lb, gsz, mod_off):
    b, seq, _ = x1.shape
    r = bb * lb
    nl = seq // lb
    tiles_per_group = gsz // r

    def y_map(k):
        def m(i, j):
            t = i * nl + j
            return (t // tiles_per_group, k * tiles_per_group + t % tiles_per_group, 0)
        return m

    return pl.pallas_call(
        functools.partial(_combine_kernel, bb=bb, lb=lb),
        grid=(b // bb, nl),
        in_specs=[pl.BlockSpec((bb, lb, D_MODEL), lambda i, j: (i, j, 0)),
                  pl.BlockSpec((None, bb, 6, D_MODEL), lambda i, j: (l, mod_off // bb + i, 0, 0)),
                  pl.BlockSpec((1, r, HALF), y_map(0)),
                  pl.BlockSpec((1, r, HALF), y_map(1)),
                  pl.BlockSpec((r, LANES), lambda i, j: (i * nl + j, 0)),
                  _layer_spec(w["ln2_g"], l), _layer_spec(w["ln2_b"], l)],
        out_specs=pl.BlockSpec((bb, lb, D_MODEL), lambda i, j: (i, j, 0)),
        out_shape=jax.ShapeDtypeStruct((b, seq, D_MODEL), f32),
        compiler_params=_params(),
    )(x1, ada, ys, ys, route, w["ln2_g"], w["ln2_b"])


def _moe(x1, ada, xpk, route, w, l, *, bb, lb, gsz, mod_off):
    n = xpk.shape[0]
    n_groups = n // gsz
    max_blocks = (2 * gsz + N_EXPERTS * (MOE_BLOCK - 1)) // MOE_BLOCK
    dest, counts = _rank_call(route, n_groups=n_groups, gsz=gsz)
    cnt = counts[:, 0, :N_EXPERTS].astype(jnp.int32)
    end_blk = jnp.cumsum((cnt + MOE_BLOCK - 1) // MOE_BLOCK, axis=-1)
    n_valid = end_blk[:, -1]
    blk = jnp.arange(max_blocks, dtype=jnp.int32)
    block_e = jnp.sum(blk[None, :, None] >= end_blk[:, None, :], axis=-1).astype(jnp.int32)
    block_e = jnp.minimum(block_e, N_EXPERTS - 1).reshape(n_groups * max_blocks)
    ys = _expert_call(xpk, block_e, n_valid, dest, w, l, n_groups=n_groups, gsz=gsz, max_blocks=max_blocks)
    return _combine_call(x1, ada, ys, route, w, l, bb=bb, lb=lb, gsz=gsz, mod_off=mod_off)


def _win_kernel(w_ref, wa_ref, wgl_ref):
    wa_ref[0] = w_ref[0, :, 0:WA_WIDTH].astype(bf16)
    wgl_ref[0] = w_ref[0, :, GATE_OFF:GATE_OFF + N_BRANCH * D_MODEL].astype(bf16)


def _win_call(w_in):
    rows = 128
    width = w_in.shape[-1]
    return pl.pallas_call(
        _win_kernel, grid=(DEPTH, D_MODEL // rows),
        in_specs=[pl.BlockSpec((1, rows, width), lambda l, i: (l, i, 0))],
        out_specs=(pl.BlockSpec((1, rows, WA_WIDTH), lambda l, i: (l, i, 0)),
                   pl.BlockSpec((1, rows, N_BRANCH * D_MODEL), lambda l, i: (l, i, 0))),
        out_shape=(jax.ShapeDtypeStruct((DEPTH, D_MODEL, WA_WIDTH), bf16),
                   jax.ShapeDtypeStruct((DEPTH, D_MODEL, N_BRANCH * D_MODEL), bf16)),
        compiler_params=_params(),
    )(w_in)


def _prep_weights(p):
    wa, w_gl = _win_call(p["w_in"])
    lane_pad = lambda v: jnp.pad(v, ((0, 0), (0, LANES - v.shape[1])))[:, None, :]
    row = lambda v: v[:, None, :]
    wr = jnp.pad(jnp.concatenate([p["router_g"], p["router_e"]], axis=-1),
                 ((0, 0), (0, 0), (0, LANES - 4 - N_EXPERTS)))
    wr_hi = wr.astype(bf16)
    wr_lo = (wr - wr_hi.astype(f32)).astype(bf16)
    return dict(
        wa=wa, w_gl=w_gl,
        sgu_ln_g=row(p["sgu_ln_g"]), sgu_ln_b=row(p["sgu_ln_b"]),
        conv_w=p["conv_w"], conv_bias=row(p["conv_bias"]),
        conv_ln_g=row(p["conv_ln_g"]), conv_ln_b=row(p["conv_ln_b"]),
        pool_w=p["pool_w"], pool_scale=row(p["pool_scale"]),
        ssm_conv_w=p["ssm_conv_w"], ssm_conv_b=row(p["ssm_conv_b"]),
        dt_bias=lane_pad(p["ssm_dt_bias"]), a_log=lane_pad(p["ssm_a_log"]),
        d_skip=row(jnp.repeat(p["ssm_d"], SSM_HEADDIM, axis=-1)), norm_g=row(p["ssm_norm_g"]),
        w_br=jnp.concatenate([p["w_br_a"], p["w_br_b"], p["w_br_c"], p["w_br_d"]], axis=1).astype(bf16),
        w_o=p["w_o"].astype(bf16),
        ln1_g=row(p["ln1_g"]), ln1_b=row(p["ln1_b"]),
        w_router=jnp.stack([wr_hi, wr_lo], axis=1),
        w_e_gate=p["w_e_gate"].astype(bf16), w_e_up=p["w_e_up"].astype(bf16), w_e_down=p["w_e_down"].astype(bf16),
        ln2_g=row(p["ln2_g"]), ln2_b=row(p["ln2_b"]),
    )


def _prep_sgu(p, q, r):
    nb = r // q
    idx = np.arange(r)
    keep = (idx[:, None] // q == idx[None, :] // q) & (idx[None, :] % q <= idx[:, None] % q)
    sgu_w = jnp.tile(p["sgu_w"][:, :, :q, :q], (1, 1, nb, nb))
    sgu_w = jnp.where(keep, sgu_w, 0.0).astype(bf16)
    bias = jnp.tile(p["sgu_b"][:, :, :q], (1, 1, nb))
    bias = jnp.repeat(jnp.swapaxes(bias, 1, 2), LANES, axis=-1)
    return dict(sgu_w=sgu_w, sgu_bias_rows=bias)


PROMPT_MIX = (1, 128)
SAMPLE_MIX = (8, 8)
PROMPT_TOK = (1, 256)
SAMPLE_TOK = (32, 8)
PROMPT_GROUP = 4096
SAMPLE_GROUP = 1024


def kernel(x_prompt, x_sample, state_conv, state_pool, state_ssm_conv, state_ssm, c_prompt, c_sample,
           w_ada, b_ada, w_in, sgu_ln_g, sgu_ln_b, sgu_w, sgu_b, conv_w, conv_bias, conv_ln_g, conv_ln_b,
           pool_w, pool_scale, ssm_conv_w, ssm_conv_b, ssm_dt_bias, ssm_a_log, ssm_d, ssm_norm_g,
           w_br_a, w_br_b, w_br_c, w_br_d, w_o, ln1_g, ln1_b, router_g, router_e, w_e_gate, w_e_up,
           w_e_down, ln2_g, ln2_b):
    return _forward(x_prompt, x_sample, state_conv, state_pool, state_ssm_conv, state_ssm, c_prompt, c_sample,
                    w_ada, b_ada, w_in, sgu_ln_g, sgu_ln_b, sgu_w, sgu_b, conv_w, conv_bias, conv_ln_g, conv_ln_b,
                    pool_w, pool_scale, ssm_conv_w, ssm_conv_b, ssm_dt_bias, ssm_a_log, ssm_d, ssm_norm_g,
                    w_br_a, w_br_b, w_br_c, w_br_d, w_o, ln1_g, ln1_b, router_g, router_e, w_e_gate, w_e_up,
                    w_e_down, ln2_g, ln2_b)


def _forward(x_prompt, x_sample, state_conv, state_pool, state_ssm_conv, state_ssm, c_prompt, c_sample,
             w_ada, b_ada, w_in, sgu_ln_g, sgu_ln_b, sgu_w, sgu_b, conv_w, conv_bias, conv_ln_g, conv_ln_b,
             pool_w, pool_scale, ssm_conv_w, ssm_conv_b, ssm_dt_bias, ssm_a_log, ssm_d, ssm_norm_g,
             w_br_a, w_br_b, w_br_c, w_br_d, w_o, ln1_g, ln1_b, router_g, router_e, w_e_gate, w_e_up,
             w_e_down, ln2_g, ln2_b, prompt_group=PROMPT_GROUP, sample_group=SAMPLE_GROUP,
             sample_tok=SAMPLE_TOK):
    p = dict(w_in=w_in, sgu_ln_g=sgu_ln_g, sgu_ln_b=sgu_ln_b, sgu_w=sgu_w, sgu_b=sgu_b, conv_w=conv_w,
             conv_bias=conv_bias, conv_ln_g=conv_ln_g, conv_ln_b=conv_ln_b, pool_w=pool_w, pool_scale=pool_scale,
             ssm_conv_w=ssm_conv_w, ssm_conv_b=ssm_conv_b, ssm_dt_bias=ssm_dt_bias, ssm_a_log=ssm_a_log,
             ssm_d=ssm_d, ssm_norm_g=ssm_norm_g, w_br_a=w_br_a, w_br_b=w_br_b, w_br_c=w_br_c, w_br_d=w_br_d,
             w_o=w_o, ln1_g=ln1_g, ln1_b=ln1_b, router_g=router_g, router_e=router_e, ln2_g=ln2_g, ln2_b=ln2_b,
             w_e_gate=w_e_gate, w_e_up=w_e_up, w_e_down=w_e_down)
    bp = x_prompt.shape[0]
    bs = x_sample.shape[0]
    w = _prep_weights(p)
    ada = _ada_call(jnp.concatenate([c_sample, c_prompt], axis=0), w_ada, b_ada)
    ada = ada.reshape(DEPTH, bs + bp, 6, D_MODEL)

    groups = [
        dict(x=x_prompt, mix=PROMPT_MIX, tok=PROMPT_TOK, gsz=prompt_group, start=0, mod_off=bs, vn=None,
             states=tuple(jnp.zeros((DEPTH, bp) + s, f32) for s in STATE_SHAPES)),
        dict(x=x_sample, mix=SAMPLE_MIX, tok=sample_tok, gsz=sample_group, start=PAST_LEN, mod_off=0,
             vn=jnp.zeros((DEPTH, bs, x_sample.shape[1], A_WIDTH), f32),
             states=(state_conv, state_pool, state_ssm_conv, state_ssm.reshape(DEPTH, bs, D_WIDTH, SSM_STATE))),
    ]
    for g in groups:
        g["w"] = dict(w, **_prep_sgu(p, g["mix"][1], g["mix"][0] * g["mix"][1]))
    for l in range(DEPTH):
        for g in groups:
            bb, lb = g["mix"]
            ycat, g["states"], g["vn"] = _mixer_call(
                g["x"], ada, g["states"], g["vn"], g["w"], l, bb=bb, lb=lb, start_pos=g["start"],
                mod_off=g["mod_off"])
            bb, lb = g["tok"]
            x1, xpk, route = _merge_call(g["x"], ada, ycat, g["w"], l, bb=bb, lb=lb, mod_off=g["mod_off"])
            g["x"] = _moe(x1, ada, xpk, route, g["w"], l, bb=bb, lb=lb, gsz=g["gsz"], mod_off=g["mod_off"])

    def states_out(g, b):
        c, pl_, sc, sm = g["states"]
        return c, pl_, sc, sm.reshape(DEPTH, b, SSM_HEADS, SSM_HEADDIM, SSM_STATE)

    return ((groups[0]["x"], groups[1]["x"]) + states_out(groups[0], bp) + states_out(groups[1], bs)
            + (groups[1]["vn"],))
```

```python
import functools
import math

import jax
import jax.numpy as jnp
import numpy as np
from jax import lax
from jax.experimental import pallas as pl
from jax.experimental.pallas import tpu as pltpu

D_MODEL = 1024
DEPTH = 4
A_WIDTH = 512
A_GROUPS = 4
B_WIDTH = 512
CONV_W = 31
C_WIDTH = 512
POOL_WINDOWS = (2, 4, 8, 16)
POOL_BUF = 15
SSM_HEADS = 12
SSM_HEADDIM = 64
D_WIDTH = SSM_HEADS * SSM_HEADDIM
SSM_GROUPS = 4
SSM_STATE = 128
SSM_CONV = 4
XBC_WIDTH = D_WIDTH + 2 * SSM_GROUPS * SSM_STATE
N_BRANCH = 4
E_PER_GROUP = 8
N_EXPERTS = 32
EXPERT_FF = 512
MOE_BLOCK = 128
DN_ALPHA = (2 * DEPTH) ** 0.25
LN_EPS = 1e-5
PAST_LEN = 16384
Y_WIDTH = A_WIDTH + B_WIDTH + C_WIDTH + D_WIDTH
HALF = D_MODEL // 2

OFF_U, OFF_V, OFF_GA, OFF_GG, OFF_P, OFF_Z, OFF_XBC, OFF_DT = 0, 512, 1024, 1536, 2048, 2560, 3328, 5120
GATE_OFF = OFF_DT + SSM_HEADS
WA_WIDTH = 5248
LANES = 128
SUBLANES = 8
HIST_CONV = 32
HIST_POOL = 16
HIST_SCONV = 8
VMEM_LIMIT = 56 * 1024 * 1024
RANK_TILE = 512

f32 = jnp.float32
bf16 = jnp.bfloat16


def _dot(a, b):
    return jnp.dot(a, b, preferred_element_type=f32)


def _dot_nt(a, b):
    return lax.dot_general(a, b, (((1,), (1,)), ((), ())), preferred_element_type=f32)


def _dot_tn(a, b):
    return lax.dot_general(a, b, (((0,), (0,)), ((), ())), preferred_element_type=f32)


def _split3(x):
    h1 = x.astype(bf16)
    r1 = x - h1.astype(f32)
    h2 = r1.astype(bf16)
    r2 = r1 - h2.astype(f32)
    return h1, h2, r2.astype(bf16)


def _sigmoid(x):
    return 0.5 * (jnp.tanh(0.5 * x) + 1.0)


def _silu(x):
    return x * _sigmoid(x)


def _gelu(x):
    return 0.5 * x * (1.0 + jnp.tanh(math.sqrt(2.0 / math.pi) * (x + 0.044715 * (x * x * x))))


def _softplus(x):
    return jnp.maximum(x, 0.0) + jnp.log1p(jnp.exp(-jnp.abs(x)))


def _layer_norm(x, g, b):
    mu = jnp.mean(x, axis=-1, keepdims=True)
    xc = x - mu
    var = jnp.mean(xc * xc, axis=-1, keepdims=True)
    return xc * lax.rsqrt(var + LN_EPS) * g + b


def _layer_spec(a, l):
    nd = a.ndim
    return pl.BlockSpec((None,) + a.shape[1:], lambda *_: (l,) + (0,) * (nd - 1))


def _params(n_axes=2):
    return pltpu.CompilerParams(dimension_semantics=("arbitrary",) * n_axes, vmem_limit_bytes=VMEM_LIMIT)


def _ada_kernel(c_ref, w_ref, b_ref, o_ref):
    a = _silu(c_ref[...]).astype(bf16)
    o_ref[0] = _dot(a, w_ref[0].astype(bf16)) + b_ref[0]


def _ada_call(c_all, w_ada, b_ada):
    n = c_all.shape[0]
    return pl.pallas_call(
        _ada_kernel,
        grid=(DEPTH, 6),
        in_specs=[
            pl.BlockSpec((n, D_MODEL), lambda l, j: (0, 0)),
            pl.BlockSpec((1, D_MODEL, D_MODEL), lambda l, j: (l, 0, j)),
            pl.BlockSpec((1, 1, D_MODEL), lambda l, j: (l, 0, j)),
        ],
        out_specs=pl.BlockSpec((1, n, D_MODEL), lambda l, j: (l, 0, j)),
        out_shape=jax.ShapeDtypeStruct((DEPTH, n, 6 * D_MODEL), f32),
        compiler_params=_params(),
    )(c_all, w_ada, b_ada.reshape(DEPTH, 1, 6 * D_MODEL))


def _mixer_kernel(x_ref, mod_ref, stc_ref, stp_ref, sts_ref, stm_ref, *rest, bb, lb, start_pos, emit_vn):
    if emit_vn:
        rest = rest[1:]
    (wa_ref, sgw_ref, sgb_ref, slg_ref, slb_ref, cw_ref, cb_ref, clg_ref, clb_ref, pw_ref, ps_ref,
     scw_ref, scb_ref, dtb_ref, alog_ref, dsk_ref, ng_ref, y_ref, nc_ref, np_ref, ns_ref, nm_ref) = rest[:22]
    vn_ref = rest[22] if emit_vn else None
    hbuf, hsh, pbuf, xbuf, sst, cbuf = rest[-6:]
    r = bb * lb
    nb, q = bb, lb
    lq = q.bit_length() - 1
    j = pl.program_id(1)

    @pl.when(j == 0)
    def _():
        hbuf[:, HIST_CONV - (CONV_W - 1):HIST_CONV, :] = stc_ref[...]
        pbuf[:, HIST_POOL - POOL_BUF:HIST_POOL, :] = stp_ref[...]
        xbuf[:, HIST_SCONV - (SSM_CONV - 1):HIST_SCONV, :] = sts_ref[...]
        sst[...] = stm_ref[...]

    x = x_ref[...]
    xm = (x * (1.0 + mod_ref[:, 1:2, :]) + mod_ref[:, 0:1, :]).reshape(r, D_MODEL).astype(bf16)

    row_i = lax.broadcasted_iota(jnp.int32, (r, r), 0)
    col_i = lax.broadcasted_iota(jnp.int32, (r, r), 1)
    same = (row_i >> lq) == (col_i >> lq)
    causal = same & ((col_i & (q - 1)) <= (row_i & (q - 1)))
    causal_t = same & ((row_i & (q - 1)) <= (col_i & (q - 1)))

    ga = _dot(xm, wa_ref[:, OFF_GA:OFF_GA + B_WIDTH])
    gg = _dot(xm, wa_ref[:, OFF_GG:OFF_GG + B_WIDTH])
    h = ga * _sigmoid(gg)
    hbuf[:, HIST_CONV:HIST_CONV + lb, :] = h.reshape(bb, lb, B_WIDTH)
    base = HIST_CONV - (CONV_W - 1)
    span = HIST_CONV + lb - SUBLANES
    for s in range(1, SUBLANES):
        hsh[s - 1, :, 0:span, :] = hbuf[:, s:s + span, :]
    lc = min(lb, 64)
    for rc in range(lb // lc):
        acc = jnp.zeros((bb, lc, B_WIDTH), f32) + cb_ref[...]
        for k in range(CONV_W):
            o = base + k + rc * lc
            s, a = o % SUBLANES, o - o % SUBLANES
            tap = hbuf[:, a:a + lc, :] if s == 0 else hsh[s - 1, :, a:a + lc, :]
            acc = acc + tap * cw_ref[k:k + 1, :]
        cbuf[:, rc * lc:(rc + 1) * lc, :] = acc


    z = _dot(xm, wa_ref[:, OFF_Z:OFF_Z + D_WIDTH])
    xbc = _dot(xm, wa_ref[:, OFF_XBC:OFF_XBC + XBC_WIDTH])
    dtr = _dot(xm, wa_ref[:, OFF_DT:OFF_DT + LANES])
    xbuf[:, HIST_SCONV:HIST_SCONV + lb, :] = xbc.reshape(bb, lb, XBC_WIDTH)
    sbase = HIST_SCONV - (SSM_CONV - 1)
    acc = jnp.zeros((bb, lb, XBC_WIDTH), f32) + scb_ref[...]
    for k in range(SSM_CONV):
        acc = acc + xbuf[:, sbase + k:sbase + k + lb, :] * scw_ref[k:k + 1, :]
    xc = _silu(acc).reshape(r, XBC_WIDTH)
    new_s = xbuf[:, lb + sbase:lb + HIST_SCONV, :]
    ns_ref[...] = new_s
    xbuf[:, sbase:HIST_SCONV, :] = new_s


    u = _gelu(_dot(xm, wa_ref[:, OFF_U:OFF_U + A_WIDTH]))
    v = _gelu(_dot(xm, wa_ref[:, OFF_V:OFF_V + A_WIDTH]))
    pin = _dot(xm, wa_ref[:, OFF_P:OFF_P + C_WIDTH])
    vn = _layer_norm(v, slg_ref[...], slb_ref[...])
    if emit_vn:
        vn_ref[...] = vn.reshape(bb, lb, A_WIDTH)
    vnb = vn.astype(bf16)
    for g in range(A_GROUPS):
        sl = slice(g * LANES, (g + 1) * LANES)
        s = _dot(sgw_ref[g], vnb[:, sl]) + sgb_ref[:, sl]
        y_ref[:, sl] = (u[:, sl] * s).astype(bf16)

    yb = _silu(_layer_norm(cbuf[...].reshape(r, B_WIDTH), clg_ref[...], clb_ref[...]))
    y_ref[:, A_WIDTH:A_WIDTH + B_WIDTH] = yb.astype(bf16)
    new_c = hbuf[:, lb + base:lb + HIST_CONV, :]
    nc_ref[...] = new_c
    hbuf[:, base:HIST_CONV, :] = new_c

    pbuf[:, HIST_POOL:HIST_POOL + lb, :] = pin.reshape(bb, lb, C_WIDTH)
    pos = (start_pos + j * lb + lax.broadcasted_iota(jnp.int32, (bb, lb, LANES), 1)).astype(f32)
    for gi, win in enumerate(POOL_WINDOWS):
        sl = slice(gi * LANES, (gi + 1) * LANES)
        acc = pbuf[:, HIST_POOL:HIST_POOL + lb, sl]
        for i in range(1, win):
            acc = acc + pbuf[:, HIST_POOL - i:HIST_POOL - i + lb, sl]
        cnt = jnp.minimum(float(win), pos + 1.0)
        mixed = (acc / cnt - pbuf[:, HIST_POOL:HIST_POOL + lb, sl]).reshape(r, LANES)
        out = _dot(mixed.astype(bf16), pw_ref[gi].astype(bf16)) * ps_ref[:, sl]
        y_ref[:, A_WIDTH + B_WIDTH + gi * LANES:A_WIDTH + B_WIDTH + (gi + 1) * LANES] = out.astype(bf16)
    new_p = pbuf[:, lb + HIST_POOL - POOL_BUF:lb + HIST_POOL, :]
    np_ref[...] = new_p
    pbuf[:, HIST_POOL - POOL_BUF:HIST_POOL, :] = new_p

    dt = _softplus(dtr + dtb_ref[...])
    da = dt * (-jnp.exp(alog_ref[...]))
    d1, d2, d3 = _split3(da)
    mc = jnp.where(causal, 1.0, 0.0).astype(bf16)
    mct = jnp.where(causal_t, 1.0, 0.0).astype(bf16)
    ms = jnp.where(same, 1.0, 0.0).astype(bf16)
    acs = _dot(mc, d1) + _dot(mc, d2) + _dot(mc, d3)
    acs_t = _dot_tn(d1, mct) + _dot_tn(d2, mct) + _dot_tn(d3, mct)
    tot = _dot(ms, d1) + _dot(ms, d2) + _dot(ms, d3)

    lane = lax.broadcasted_iota(jnp.int32, (r, LANES), 1)
    lo_half = lane < SSM_HEADDIM
    cbs = []
    for g in range(SSM_GROUPS):
        bg = xc[:, D_WIDTH + g * SSM_STATE:D_WIDTH + (g + 1) * SSM_STATE].astype(bf16)
        cg = xc[:, D_WIDTH + (SSM_GROUPS + g) * SSM_STATE:D_WIDTH + (SSM_GROUPS + g + 1) * SSM_STATE].astype(bf16)
        cbs.append((bg, cg, _dot_nt(cg, bg)))
    rep = SSM_HEADS // SSM_GROUPS
    rowseq = lax.broadcasted_iota(jnp.int32, (r, LANES), 0) >> lq
    ysq = jnp.zeros((r, 1), f32)
    ypairs = []
    for i in range(SSM_HEADS // 2):
        h0, h1 = 2 * i, 2 * i + 1
        g0, g1 = h0 // rep, h1 // rep
        sl = slice(i * LANES, (i + 1) * LANES)
        xs = xc[:, sl]
        dtp = jnp.where(lo_half, dt[:, h0:h0 + 1], dt[:, h1:h1 + 1])
        acp = jnp.where(lo_half, acs[:, h0:h0 + 1], acs[:, h1:h1 + 1])
        totp = jnp.where(lo_half, tot[:, h0:h0 + 1], tot[:, h1:h1 + 1])
        xd = xs * dtp
        xdb = xd.astype(bf16)
        res = []
        for hh, gg_ in ((h0, g0), (h1, g1)):
            diff = acs[:, hh:hh + 1] - acs_t[hh:hh + 1, :]
            dec = jnp.exp(jnp.where(causal, diff, -1e30))
            res.append(_dot((cbs[gg_][2] * dec).astype(bf16), xdb))
        ydiag = jnp.where(lo_half, res[0], res[1])
        s_all = sst[:, i * LANES:(i + 1) * LANES, :].reshape(nb * LANES, SSM_STATE)
        s_bf = s_all.astype(bf16)
        full0 = _dot_nt(cbs[g0][1], s_bf)
        full1 = full0 if g1 == g0 else _dot_nt(cbs[g1][1], s_bf)
        yoff = jnp.zeros((r, LANES), f32)
        for b in range(nb):
            blk = jnp.where(lo_half, full0[:, b * LANES:(b + 1) * LANES], full1[:, b * LANES:(b + 1) * LANES])
            yoff = yoff + (jnp.where(rowseq == b, blk, 0.0) if nb > 1 else blk)
        yp = ydiag + yoff * jnp.exp(acp) + dsk_ref[:, sl] * xs
        yp = yp * _silu(z[:, sl])
        ysq = ysq + jnp.sum(yp * yp, axis=-1, keepdims=True)
        ypairs.append(yp)
        xdd = xd * jnp.exp(totp - acp)
        if nb > 1:
            colblk = lax.broadcasted_iota(jnp.int32, (r, nb * LANES), 1) >> 7
            rowblk = lax.broadcasted_iota(jnp.int32, (r, nb * LANES), 0) >> lq
            xblk = jnp.where(colblk == rowblk, jnp.concatenate([xdd] * nb, axis=1), 0.0).astype(bf16)
        else:
            xblk = xdd.astype(bf16)
        upd0 = _dot_tn(xblk, cbs[g0][0])
        if g1 != g0:
            upd1 = _dot_tn(xblk, cbs[g1][0])
            prow = lax.broadcasted_iota(jnp.int32, (nb * LANES, SSM_STATE), 0) & (LANES - 1)
            upd0 = jnp.where(prow < SSM_HEADDIM, upd0, upd1)
        upd0 = upd0.reshape(nb, LANES, SSM_STATE)
        for b in range(nb):
            for hh, half in ((h0, 0), (h1, 1)):
                fac = jnp.exp(jnp.broadcast_to(tot[b * q:b * q + 1, hh:hh + 1], (SSM_HEADDIM, SSM_STATE)))
                rs = slice(i * LANES + half * SSM_HEADDIM, i * LANES + (half + 1) * SSM_HEADDIM)
                sst[b, rs, :] = fac * sst[b, rs, :] + upd0[b, half * SSM_HEADDIM:(half + 1) * SSM_HEADDIM, :]
    rms = lax.rsqrt(ysq * (1.0 / D_WIDTH) + LN_EPS)
    for i in range(SSM_HEADS // 2):
        sl = slice(i * LANES, (i + 1) * LANES)
        yd = ypairs[i] * rms * ng_ref[:, sl]
        o = A_WIDTH + B_WIDTH + C_WIDTH + i * LANES
        y_ref[:, o:o + LANES] = yd.astype(bf16)
    nm_ref[...] = sst[...]


def _merge_kernel(x_ref, mod_ref, y_ref, wg_ref, wbr_ref, wo_ref, lg_ref, lb_ref, wr_ref,
                  x1_ref, xpk_ref, rt_ref, *, bb, lb):
    r = bb * lb
    x = x_ref[...]
    xm = (x * (1.0 + mod_ref[:, 1:2, :]) + mod_ref[:, 0:1, :]).reshape(r, D_MODEL).astype(bf16)
    y = y_ref[...]
    offs = (0, A_WIDTH, A_WIDTH + B_WIDTH, A_WIDTH + B_WIDTH + C_WIDTH, Y_WIDTH)
    merged = jnp.zeros((r, D_MODEL), f32)
    for k in range(N_BRANCH):
        gate = _sigmoid(_dot(xm, wg_ref[:, k * D_MODEL:(k + 1) * D_MODEL]))
        merged = merged + gate * _dot(y[:, offs[k]:offs[k + 1]], wbr_ref[offs[k]:offs[k + 1], :])
    mix = _dot(merged.astype(bf16), wo_ref[...]).reshape(bb, lb, D_MODEL)
    x1 = _layer_norm(DN_ALPHA * x + mod_ref[:, 2:3, :] * mix, lg_ref[...], lb_ref[...])
    x1_ref[...] = x1
    xm2 = (x1 * (1.0 + mod_ref[:, 4:5, :]) + mod_ref[:, 3:4, :]).reshape(r, D_MODEL)
    xpk_ref[...] = _pack_bf16_pair(xm2[:, :HALF], xm2[:, HALF:])
    rt_ref[...] = _route(xm2, wr_ref)


def _route(xm2, wr_ref):
    r = xm2.shape[0]
    a1, a2, _ = _split3(xm2)
    lg = _dot(a1, wr_ref[0]) + (_dot(a2, wr_ref[0]) + _dot(a1, wr_ref[1]))
    lane = lax.broadcasted_iota(jnp.int32, (r, LANES), 1).astype(f32)
    neg, big = -1e30, 1e6
    mask_g = lane < 4.0
    lgg = jnp.where(mask_g, lg, neg)
    mg = jnp.max(lgg, axis=-1, keepdims=True)
    grp = jnp.min(jnp.where(lgg == mg, lane, big), axis=-1, keepdims=True)
    p_grp = 1.0 / jnp.sum(jnp.where(mask_g, jnp.exp(lgg - mg), 0.0), axis=-1, keepdims=True)
    lo = 4.0 + grp * E_PER_GROUP
    mask_e = (lane >= lo) & (lane < lo + E_PER_GROUP)
    lge = jnp.where(mask_e, lg, neg)
    v1 = jnp.max(lge, axis=-1, keepdims=True)
    i1 = jnp.min(jnp.where(lge == v1, lane, big), axis=-1, keepdims=True)
    lge2 = jnp.where(lane == i1, neg, lge)
    v2 = jnp.max(lge2, axis=-1, keepdims=True)
    i2 = jnp.min(jnp.where(lge2 == v2, lane, big), axis=-1, keepdims=True)
    e21 = jnp.exp(v2 - v1)
    w1 = p_grp / (1.0 + e21)
    w2 = p_grp * e21 / (1.0 + e21)
    return jnp.where(lane == 0.0, i1 - 4.0,
                     jnp.where(lane == 1.0, i2 - 4.0,
                               jnp.where(lane == 2.0, w1, jnp.where(lane == 3.0, w2, 0.0))))


MIXER_WEIGHTS = ("wa", "sgu_w", "sgu_bias_rows", "sgu_ln_g", "sgu_ln_b", "conv_w", "conv_bias", "conv_ln_g",
                 "conv_ln_b", "pool_w", "pool_scale", "ssm_conv_w", "ssm_conv_b", "dt_bias", "a_log", "d_skip",
                 "norm_g")
MERGE_WEIGHTS = ("w_gl", "w_br", "w_o", "ln1_g", "ln1_b", "w_router")
BIG_WEIGHTS = ("wa", "w_gl", "w_br", "w_o")
STATE_SHAPES = ((CONV_W - 1, B_WIDTH), (POOL_BUF, C_WIDTH), (SSM_CONV - 1, XBC_WIDTH), (D_WIDTH, SSM_STATE))


def _weight_spec(w, name, l):
    spec = _layer_spec(w[name], l)
    if name in BIG_WEIGHTS:
        spec = pl.BlockSpec(spec.block_shape, spec.index_map, pipeline_mode=pl.Buffered(1))
    return spec


def _mixer_call(x, ada, states, vn_buf, w, l, *, bb, lb, start_pos, mod_off):
    b, seq, _ = x.shape
    r = bb * lb
    nl = seq // lb
    emit_vn = vn_buf is not None
    kern = functools.partial(_mixer_kernel, bb=bb, lb=lb, start_pos=start_pos, emit_vn=emit_vn)

    def state_spec(shape):
        return pl.BlockSpec((None, bb) + shape, lambda i, j: (l, i) + (0,) * len(shape))

    weights = [w[n] for n in MIXER_WEIGHTS]
    extra_in = [vn_buf] if emit_vn else []
    in_specs = ([pl.BlockSpec((bb, lb, D_MODEL), lambda i, j: (i, j, 0)),
                 pl.BlockSpec((None, bb, 6, D_MODEL), lambda i, j: (l, mod_off // bb + i, 0, 0))]
                + [state_spec(s) for s in STATE_SHAPES]
                + [pl.BlockSpec(memory_space=pl.ANY) for _ in extra_in]
                + [_weight_spec(w, n, l) for n in MIXER_WEIGHTS])
    n_st = len(STATE_SHAPES)
    out_shape = [jax.ShapeDtypeStruct((b * seq, Y_WIDTH), bf16)]
    out_shape += [jax.ShapeDtypeStruct(s.shape, f32) for s in states]
    out_specs = [pl.BlockSpec((r, Y_WIDTH), lambda i, j: (i * nl + j, 0))] + [state_spec(s) for s in STATE_SHAPES]
    aliases = {2 + k: 1 + k for k in range(n_st)}
    if emit_vn:
        out_shape.append(jax.ShapeDtypeStruct(vn_buf.shape, f32))
        out_specs.append(pl.BlockSpec((None, bb, lb, A_WIDTH), lambda i, j: (l, i, j, 0)))
        aliases[2 + n_st] = 1 + n_st
    scratch = [
        pltpu.VMEM((bb, HIST_CONV + lb, B_WIDTH), f32),
        pltpu.VMEM((SUBLANES - 1, bb, HIST_CONV + lb, B_WIDTH), f32),
        pltpu.VMEM((bb, HIST_POOL + lb, C_WIDTH), f32),
        pltpu.VMEM((bb, HIST_SCONV + lb, XBC_WIDTH), f32),
        pltpu.VMEM((bb, D_WIDTH, SSM_STATE), f32),
        pltpu.VMEM((bb, lb, B_WIDTH), f32),
    ]
    res = pl.pallas_call(
        kern, grid=(b // bb, nl), in_specs=in_specs, out_specs=out_specs, out_shape=out_shape,
        scratch_shapes=scratch, input_output_aliases=aliases, compiler_params=_params(),
    )(x, ada, *states, *extra_in, *weights)
    return res[0], tuple(res[1:1 + n_st]), (res[1 + n_st] if emit_vn else None)


def _merge_call(x, ada, ycat, w, l, *, bb, lb, mod_off):
    b, seq, _ = x.shape
    r = bb * lb
    nl = seq // lb
    n = b * seq
    weights = [w[k] for k in MERGE_WEIGHTS]
    return pl.pallas_call(
        functools.partial(_merge_kernel, bb=bb, lb=lb), grid=(b // bb, nl),
        in_specs=[pl.BlockSpec((bb, lb, D_MODEL), lambda i, j: (i, j, 0)),
                  pl.BlockSpec((None, bb, 6, D_MODEL), lambda i, j: (l, mod_off // bb + i, 0, 0)),
                  pl.BlockSpec((r, Y_WIDTH), lambda i, j: (i * nl + j, 0))]
        + [_weight_spec(w, k, l) for k in MERGE_WEIGHTS],
        out_specs=(pl.BlockSpec((bb, lb, D_MODEL), lambda i, j: (i, j, 0)),
                   pl.BlockSpec((r, HALF), lambda i, j: (i * nl + j, 0)),
                   pl.BlockSpec((r, LANES), lambda i, j: (i * nl + j, 0))),
        out_shape=(jax.ShapeDtypeStruct((b, seq, D_MODEL), f32),
                   jax.ShapeDtypeStruct((n, HALF), jnp.uint32),
                   jax.ShapeDtypeStruct((n, LANES), f32)),
        compiler_params=_params(),
    )(x, ada, ycat, *weights)


def _pack_bf16_pair(a, b):
    ua = lax.bitcast_convert_type(a.astype(bf16).astype(f32), jnp.uint32)
    ub = lax.bitcast_convert_type(b.astype(bf16).astype(f32), jnp.uint32)
    return (ua & jnp.uint32(0xFFFF0000)) | (ub >> 16)


def _unpack_bf16_pair(u):
    hi = lax.bitcast_convert_type(u & jnp.uint32(0xFFFF0000), f32)
    lo = lax.bitcast_convert_type(u << 16, f32)
    return hi, lo


def _rank_kernel(rt_ref, dest_ref, cnt_ref, carry, start, *, t, n_tiles):
    p = pl.program_id(1)
    j = pl.program_id(2)
    rt = rt_ref[...]
    lane = lax.broadcasted_iota(jnp.int32, (t, LANES), 1).astype(f32)
    oh0 = jnp.where(lane == rt[:, 0:1], 1.0, 0.0)
    oh1 = jnp.where(lane == rt[:, 1:2], 1.0, 0.0)
    both = oh0 + oh1
    colsum = jnp.sum(both, axis=0, keepdims=True)

    @pl.when((p == 0) & (j == 0))
    def _():
        carry[...] = jnp.zeros(carry.shape, f32)

    @pl.when(p == 0)
    def _():
        carry[0:1, :] = carry[0:1, :] + colsum

    @pl.when((p == 0) & (j == n_tiles - 1))
    def _():
        c = jnp.broadcast_to(carry[0:1, :], (SUBLANES, LANES))
        cnt_ref[0] = c
        nblk = jnp.floor((c + (MOE_BLOCK - 1.0)) * (1.0 / MOE_BLOCK))
        ri = lax.broadcasted_iota(jnp.int32, (LANES, LANES), 0)
        ci = lax.broadcasted_iota(jnp.int32, (LANES, LANES), 1)
        tri = jnp.where(ri < ci, 1.0, 0.0).astype(bf16)
        start[...] = _dot(nblk.astype(bf16), tri) * float(MOE_BLOCK)
        carry[...] = jnp.zeros(carry.shape, f32)

    @pl.when(p == 1)
    def _():
        rr = lax.broadcasted_iota(jnp.int32, (t, t), 0)
        cc = lax.broadcasted_iota(jnp.int32, (t, t), 1)
        ltri = jnp.where(cc < rr, 1.0, 0.0).astype(bf16)
        base = start[0:1, :] + carry[0:1, :] + _dot(ltri, both.astype(bf16))
        d0 = jnp.sum(oh0 * base, axis=-1, keepdims=True)
        d1 = jnp.sum(oh1 * base, axis=-1, keepdims=True)
        arr = jnp.where(lane == 0.0, d0, jnp.where(lane == 1.0, d1, 0.0))
        dest_ref[0] = arr.T[0:2, :].astype(jnp.int32)
        carry[0:1, :] = carry[0:1, :] + colsum


def _rank_call(route, *, n_groups, gsz):
    t = min(RANK_TILE, gsz)
    n_tiles = gsz // t
    return pl.pallas_call(
        functools.partial(_rank_kernel, t=t, n_tiles=n_tiles),
        grid=(n_groups, 2, n_tiles),
        in_specs=[pl.BlockSpec((t, LANES), lambda g, p, j: (g * n_tiles + j, 0))],
        out_specs=(pl.BlockSpec((1, 2, t), lambda g, p, j: (g, 0, j * p)),
                   pl.BlockSpec((1, SUBLANES, LANES), lambda g, p, j: (g, 0, 0))),
        out_shape=(jax.ShapeDtypeStruct((n_groups, 2, gsz), jnp.int32),
                   jax.ShapeDtypeStruct((n_groups, SUBLANES, LANES), f32)),
        scratch_shapes=[pltpu.VMEM((SUBLANES, LANES), f32), pltpu.VMEM((SUBLANES, LANES), f32)],
        compiler_params=_params(3),
    )(route)


def _expert_kernel(be_ref, nv_ref, dest_ref, x_ref, wgu0_ref, wd0_ref, wgu1_ref, wd1_ref,
                   o_ref, xg, yg, tbl, *, gsz, max_blocks):
    g = pl.program_id(0)
    b = pl.program_id(1)
    pad_row = 2 * gsz

    nv = nv_ref[g]
    xgs = tuple(xg.at[k] for k in range(4))
    ygs = tuple(yg.at[k] for k in range(4))
    w_refs = ((wgu0_ref, wd0_ref), (wgu1_ref, wd1_ref))

    @pl.when(b == 0)
    def _():
        def init(i, c):
            tbl[i] = pad_row
            return c

        lax.fori_loop(0, max_blocks * MOE_BLOCK, init, 0, unroll=8)

        def fill(t, c):
            tbl[dest_ref[0, 0, t]] = t
            tbl[dest_ref[0, 1, t]] = gsz + t
            return c

        lax.fori_loop(0, gsz, fill, 0, unroll=8)
        o_ref[0, pad_row:pad_row + SUBLANES, :] = jnp.zeros((SUBLANES, HALF), jnp.uint32)
        yg[...] = jnp.zeros(yg.shape, jnp.uint32)

        def gather0(i, c):
            xg[0, pl.ds(i, 1), :] = x_ref[pl.ds(tbl[i] & (gsz - 1), 1), :]
            second = jnp.minimum(1, nv - 1) * MOE_BLOCK
            xg[1, pl.ds(i, 1), :] = x_ref[pl.ds(tbl[second + i] & (gsz - 1), 1), :]
            return c

        lax.fori_loop(0, MOE_BLOCK, gather0, 0, unroll=8)

    def step(par):
        nxt = 1 - par
        for t in range(2):
            kn = 2 * (b + 1) + t
            kp = 2 * (b - 1) + t
            base_n = jnp.minimum(kn, nv - 1) * MOE_BLOCK
            base_p = jnp.clip(kp, 0, nv - 1) * MOE_BLOCK
            live = (kp >= 0) & (kp < nv)
            x_nxt, y_prv = xgs[2 * nxt + t], ygs[2 * nxt + t]
            for i in range(MOE_BLOCK):
                x_nxt[i:i + 1, :] = x_ref[pl.ds(tbl[base_n + i] & (gsz - 1), 1), :]
                s = jnp.where(live, tbl[base_p + i], pad_row)
                o_ref[0, pl.ds(s, 1), :] = y_prv[i:i + 1, :]
        for t in range(2):
            wgu_ref, wd_ref = w_refs[t]
            hi, lo = _unpack_bf16_pair(xgs[2 * par + t][...])
            hi = hi.astype(bf16)
            lo = lo.astype(bf16)
            hgu = _dot(hi, wgu_ref[:HALF, :]) + _dot(lo, wgu_ref[HALF:, :])
            hid = (_silu(hgu[:, :EXPERT_FF]) * hgu[:, EXPERT_FF:]).astype(bf16)
            y = _dot(hid, wd_ref[...])
            ygs[2 * par + t][...] = _pack_bf16_pair(y[:, :HALF], y[:, HALF:])

    @pl.when((2 * b <= nv + 1) & (b % 2 == 0))
    def _():
        step(0)

    @pl.when((2 * b <= nv + 1) & (b % 2 == 1))
    def _():
        step(1)


def _expert_call(xpk, block_e, n_valid, dest, w, l, *, n_groups, gsz, max_blocks):
    assert gsz & (gsz - 1) == 0

    def w_spec(shape, t):
        return pl.BlockSpec(
            (None, None) + shape,
            lambda g, b, be, nv: (l, be[g * max_blocks + jnp.minimum(2 * b + t, max_blocks - 1)], 0, 0))

    single = pl.Buffered(1)
    w_specs = [w_spec(s, t) for t in range(2) for s in ((D_MODEL, 2 * EXPERT_FF), (EXPERT_FF, D_MODEL))]
    grid_spec = pltpu.PrefetchScalarGridSpec(
        num_scalar_prefetch=2,
        grid=(n_groups, (max_blocks + 1) // 2 + 1),
        in_specs=[
            pl.BlockSpec((1, 2, gsz), lambda g, b, be, nv: (g, 0, 0), memory_space=pltpu.SMEM),
            pl.BlockSpec((gsz, HALF), lambda g, b, be, nv: (g, 0), pipeline_mode=single),
        ] + w_specs,
        out_specs=pl.BlockSpec((1, 2 * gsz + SUBLANES, HALF), lambda g, b, be, nv: (g, 0, 0),
                               pipeline_mode=single),
        scratch_shapes=[pltpu.VMEM((4, MOE_BLOCK, HALF), jnp.uint32), pltpu.VMEM((4, MOE_BLOCK, HALF), jnp.uint32),
                        pltpu.SMEM((max_blocks * MOE_BLOCK,), jnp.int32)],
    )
    ws = (w["w_e_gate_up"], w["w_e_down"])
    return pl.pallas_call(
        functools.partial(_expert_kernel, gsz=gsz, max_blocks=max_blocks), grid_spec=grid_spec,
        out_shape=jax.ShapeDtypeStruct((n_groups, 2 * gsz + SUBLANES, HALF), jnp.uint32),
        compiler_params=_params(),
    )(block_e, n_valid, dest, xpk, *ws, *ws)


def _combine_kernel(x_ref, mod_ref, y0_ref, y1_ref, rt_ref, lg_ref, lb_ref, o_ref, *, bb, lb):
    rt = rt_ref[...]
    w0 = rt[:, 2:3]
    w1 = rt[:, 3:4]
    h0, l0 = _unpack_bf16_pair(y0_ref[0])
    h1, l1 = _unpack_bf16_pair(y1_ref[0])
    f = jnp.concatenate([w0 * h0 + w1 * h1, w0 * l0 + w1 * l1], axis=1).reshape(bb, lb, D_MODEL)
    t = DN_ALPHA * x_ref[...] + mod_ref[:, 5:6, :] * f
    o_ref[...] = _layer_norm(t, lg_ref[...], lb_ref[...])


def _combine_call(x1, ada, ys, route, w, l, *, bb, lb, gsz, mod_off):
    b, seq, _ = x1.shape
    r = bb * lb
    nl = seq // lb
    tiles_per_group = gsz // r

    def y_map(k):
        def m(i, j):
            t = i * nl + j
            return (t // tiles_per_group, k * tiles_per_group + t % tiles_per_group, 0)
        return m

    return pl.pallas_call(
        functools.partial(_combine_kernel, bb=bb, lb=lb),
        grid=(b // bb, nl),
        in_specs=[pl.BlockSpec((bb, lb, D_MODEL), lambda i, j: (i, j, 0)),
                  pl.BlockSpec((None, bb, 6, D_MODEL), lambda i, j: (l, mod_off // bb + i, 0, 0)),
                  pl.BlockSpec((1, r, HALF), y_map(0)),
                  pl.BlockSpec((1, r, HALF), y_map(1)),
                  pl.BlockSpec((r, LANES), lambda i, j: (i * nl + j, 0)),
                  _layer_spec(w["ln2_g"], l), _layer_spec(w["ln2_b"], l)],
        out_specs=pl.BlockSpec((bb, lb, D_MODEL), lambda i, j: (i, j, 0)),
        out_shape=jax.ShapeDtypeStruct((b, seq, D_MODEL), f32),
        compiler_params=_params(),
    )(x1, ada, ys, ys, route, w["ln2_g"], w["ln2_b"])


def _moe(x1, ada, xpk, route, w, l, *, bb, lb, gsz, mod_off):
    n = xpk.shape[0]
    n_groups = n // gsz
    max_blocks = (2 * gsz + N_EXPERTS * (MOE_BLOCK - 1)) // MOE_BLOCK
    dest, counts = _rank_call(route, n_groups=n_groups, gsz=gsz)
    cnt = counts[:, 0, :N_EXPERTS].astype(jnp.int32)
    end_blk = jnp.cumsum((cnt + MOE_BLOCK - 1) // MOE_BLOCK, axis=-1)
    n_valid = end_blk[:, -1]
    blk = jnp.arange(max_blocks, dtype=jnp.int32)
    block_e = jnp.sum(blk[None, :, None] >= end_blk[:, None, :], axis=-1).astype(jnp.int32)
    block_e = jnp.minimum(block_e, N_EXPERTS - 1).reshape(n_groups * max_blocks)
    ys = _expert_call(xpk, block_e, n_valid, dest, w, l, n_groups=n_groups, gsz=gsz, max_blocks=max_blocks)
    return _combine_call(x1, ada, ys, route, w, l, bb=bb, lb=lb, gsz=gsz, mod_off=mod_off)


def _win_kernel(w_ref, wa_ref, wgl_ref):
    wa_ref[0] = w_ref[0, :, 0:WA_WIDTH].astype(bf16)
    wgl_ref[0] = w_ref[0, :, GATE_OFF:GATE_OFF + N_BRANCH * D_MODEL].astype(bf16)


def _win_call(w_in):
    rows = 128
    width = w_in.shape[-1]
    return pl.pallas_call(
        _win_kernel, grid=(DEPTH, D_MODEL // rows),
        in_specs=[pl.BlockSpec((1, rows, width), lambda l, i: (l, i, 0))],
        out_specs=(pl.BlockSpec((1, rows, WA_WIDTH), lambda l, i: (l, i, 0)),
                   pl.BlockSpec((1, rows, N_BRANCH * D_MODEL), lambda l, i: (l, i, 0))),
        out_shape=(jax.ShapeDtypeStruct((DEPTH, D_MODEL, WA_WIDTH), bf16),
                   jax.ShapeDtypeStruct((DEPTH, D_MODEL, N_BRANCH * D_MODEL), bf16)),
        compiler_params=_params(),
    )(w_in)


def _prep_weights(p):
    wa, w_gl = _win_call(p["w_in"])
    lane_pad = lambda v: jnp.pad(v, ((0, 0), (0, LANES - v.shape[1])))[:, None, :]
    row = lambda v: v[:, None, :]
    wr = jnp.pad(jnp.concatenate([p["router_g"], p["router_e"]], axis=-1),
                 ((0, 0), (0, 0), (0, LANES - 4 - N_EXPERTS)))
    wr_hi = wr.astype(bf16)
    wr_lo = (wr - wr_hi.astype(f32)).astype(bf16)
    return dict(
        wa=wa, w_gl=w_gl,
        sgu_ln_g=row(p["sgu_ln_g"]), sgu_ln_b=row(p["sgu_ln_b"]),
        conv_w=p["conv_w"], conv_bias=row(p["conv_bias"]),
        conv_ln_g=row(p["conv_ln_g"]), conv_ln_b=row(p["conv_ln_b"]),
        pool_w=p["pool_w"], pool_scale=row(p["pool_scale"]),
        ssm_conv_w=p["ssm_conv_w"], ssm_conv_b=row(p["ssm_conv_b"]),
        dt_bias=lane_pad(p["ssm_dt_bias"]), a_log=lane_pad(p["ssm_a_log"]),
        d_skip=row(jnp.repeat(p["ssm_d"], SSM_HEADDIM, axis=-1)), norm_g=row(p["ssm_norm_g"]),
        w_br=jnp.concatenate([p["w_br_a"], p["w_br_b"], p["w_br_c"], p["w_br_d"]], axis=1).astype(bf16),
        w_o=p["w_o"].astype(bf16),
        ln1_g=row(p["ln1_g"]), ln1_b=row(p["ln1_b"]),
        w_router=jnp.stack([wr_hi, wr_lo], axis=1),
        w_e_gate_up=jnp.concatenate([p["w_e_gate"].astype(bf16), p["w_e_up"].astype(bf16)], axis=-1),
        w_e_down=p["w_e_down"].astype(bf16),
        ln2_g=row(p["ln2_g"]), ln2_b=row(p["ln2_b"]),
    )


def _prep_sgu(p, q, r):
    nb = r // q
    idx = np.arange(r)
    keep = (idx[:, None] // q == idx[None, :] // q) & (idx[None, :] % q <= idx[:, None] % q)
    sgu_w = jnp.tile(p["sgu_w"][:, :, :q, :q], (1, 1, nb, nb))
    sgu_w = jnp.where(keep, sgu_w, 0.0).astype(bf16)
    bias = jnp.tile(p["sgu_b"][:, :, :q], (1, 1, nb))
    bias = jnp.repeat(jnp.swapaxes(bias, 1, 2), LANES, axis=-1)
    return dict(sgu_w=sgu_w, sgu_bias_rows=bias)


PROMPT_MIX = (1, 128)
SAMPLE_MIX = (8, 8)
PROMPT_TOK = (1, 256)
SAMPLE_TOK = (32, 8)
PROMPT_GROUP = 4096
SAMPLE_GROUP = 1024


def kernel(x_prompt, x_sample, state_conv, state_pool, state_ssm_conv, state_ssm, c_prompt, c_sample,
           w_ada, b_ada, w_in, sgu_ln_g, sgu_ln_b, sgu_w, sgu_b, conv_w, conv_bias, conv_ln_g, conv_ln_b,
           pool_w, pool_scale, ssm_conv_w, ssm_conv_b, ssm_dt_bias, ssm_a_log, ssm_d, ssm_norm_g,
           w_br_a, w_br_b, w_br_c, w_br_d, w_o, ln1_g, ln1_b, router_g, router_e, w_e_gate, w_e_up,
           w_e_down, ln2_g, ln2_b):
    return _forward(x_prompt, x_sample, state_conv, state_pool, state_ssm_conv, state_ssm, c_prompt, c_sample,
                    w_ada, b_ada, w_in, sgu_ln_g, sgu_ln_b, sgu_w, sgu_b, conv_w, conv_bias, conv_ln_g, conv_ln_b,
                    pool_w, pool_scale, ssm_conv_w, ssm_conv_b, ssm_dt_bias, ssm_a_log, ssm_d, ssm_norm_g,
                    w_br_a, w_br_b, w_br_c, w_br_d, w_o, ln1_g, ln1_b, router_g, router_e, w_e_gate, w_e_up,
                    w_e_down, ln2_g, ln2_b)


def _forward(x_prompt, x_sample, state_conv, state_pool, state_ssm_conv, state_ssm, c_prompt, c_sample,
             w_ada, b_ada, w_in, sgu_ln_g, sgu_ln_b, sgu_w, sgu_b, conv_w, conv_bias, conv_ln_g, conv_ln_b,
             pool_w, pool_scale, ssm_conv_w, ssm_conv_b, ssm_dt_bias, ssm_a_log, ssm_d, ssm_norm_g,
             w_br_a, w_br_b, w_br_c, w_br_d, w_o, ln1_g, ln1_b, router_g, router_e, w_e_gate, w_e_up,
             w_e_down, ln2_g, ln2_b, prompt_group=PROMPT_GROUP, sample_group=SAMPLE_GROUP,
             sample_tok=SAMPLE_TOK):
    p = dict(w_in=w_in, sgu_ln_g=sgu_ln_g, sgu_ln_b=sgu_ln_b, sgu_w=sgu_w, sgu_b=sgu_b, conv_w=conv_w,
             conv_bias=conv_bias, conv_ln_g=conv_ln_g, conv_ln_b=conv_ln_b, pool_w=pool_w, pool_scale=pool_scale,
             ssm_conv_w=ssm_conv_w, ssm_conv_b=ssm_conv_b, ssm_dt_bias=ssm_dt_bias, ssm_a_log=ssm_a_log,
             ssm_d=ssm_d, ssm_norm_g=ssm_norm_g, w_br_a=w_br_a, w_br_b=w_br_b, w_br_c=w_br_c, w_br_d=w_br_d,
             w_o=w_o, ln1_g=ln1_g, ln1_b=ln1_b, router_g=router_g, router_e=router_e, ln2_g=ln2_g, ln2_b=ln2_b,
             w_e_gate=w_e_gate, w_e_up=w_e_up, w_e_down=w_e_down)
    bp = x_prompt.shape[0]
    bs = x_sample.shape[0]
    w = _prep_weights(p)
    ada = _ada_call(jnp.concatenate([c_sample, c_prompt], axis=0), w_ada, b_ada)
    ada = ada.reshape(DEPTH, bs + bp, 6, D_MODEL)

    groups = [
        dict(x=x_prompt, mix=PROMPT_MIX, tok=PROMPT_TOK, gsz=prompt_group, start=0, mod_off=bs, vn=None,
             states=tuple(jnp.zeros((DEPTH, bp) + s, f32) for s in STATE_SHAPES)),
        dict(x=x_sample, mix=SAMPLE_MIX, tok=sample_tok, gsz=sample_group, start=PAST_LEN, mod_off=0,
             vn=jnp.zeros((DEPTH, bs, x_sample.shape[1], A_WIDTH), f32),
             states=(state_conv, state_pool, state_ssm_conv, state_ssm.reshape(DEPTH, bs, D_WIDTH, SSM_STATE))),
    ]
    for g in groups:
        g["w"] = dict(w, **_prep_sgu(p, g["mix"][1], g["mix"][0] * g["mix"][1]))
    for l in range(DEPTH):
        for g in groups:
            bb, lb = g["mix"]
            ycat, g["states"], g["vn"] = _mixer_call(
                g["x"], ada, g["states"], g["vn"], g["w"], l, bb=bb, lb=lb, start_pos=g["start"],
                mod_off=g["mod_off"])
            bb, lb = g["tok"]
            x1, xpk, route = _merge_call(g["x"], ada, ycat, g["w"], l, bb=bb, lb=lb, mod_off=g["mod_off"])
            g["x"] = _moe(x1, ada, xpk, route, g["w"], l, bb=bb, lb=lb, gsz=g["gsz"], mod_off=g["mod_off"])

    def states_out(g, b):
        c, pl_, sc, sm = g["states"]
        return c, pl_, sc, sm.reshape(DEPTH, b, SSM_HEADS, SSM_HEADDIM, SSM_STATE)

    return ((groups[0]["x"], groups[1]["x"]) + states_out(groups[0], bp) + states_out(groups[1], bs)
            + (groups[1]["vn"],))
```

```python
import functools
import math

import jax
import jax.numpy as jnp
import numpy as np
from jax import lax
from jax.experimental import pallas as pl
from jax.experimental.pallas import tpu as pltpu

D_MODEL = 1024
DEPTH = 4
A_WIDTH = 512
A_GROUPS = 4
B_WIDTH = 512
CONV_W = 31
C_WIDTH = 512
POOL_WINDOWS = (2, 4, 8, 16)
POOL_BUF = 15
SSM_HEADS = 12
SSM_HEADDIM = 64
D_WIDTH = SSM_HEADS * SSM_HEADDIM
SSM_GROUPS = 4
SSM_STATE = 128
SSM_CONV = 4
XBC_WIDTH = D_WIDTH + 2 * SSM_GROUPS * SSM_STATE
N_BRANCH = 4
E_PER_GROUP = 8
N_EXPERTS = 32
EXPERT_FF = 512
MOE_BLOCK = 128
DN_ALPHA = (2 * DEPTH) ** 0.25
LN_EPS = 1e-5
PAST_LEN = 16384
Y_WIDTH = A_WIDTH + B_WIDTH + C_WIDTH + D_WIDTH
HALF = D_MODEL // 2

OFF_U, OFF_V, OFF_GA, OFF_GG, OFF_P, OFF_Z, OFF_XBC, OFF_DT = 0, 512, 1024, 1536, 2048, 2560, 3328, 5120
GATE_OFF = OFF_DT + SSM_HEADS
WA_WIDTH = 5248
LANES = 128
SUBLANES = 8
HIST_CONV = 32
HIST_POOL = 16
HIST_SCONV = 8
VMEM_LIMIT = 56 * 1024 * 1024
RANK_TILE = 1024

f32 = jnp.float32
bf16 = jnp.bfloat16


def _dot(a, b):
    return jnp.dot(a, b, preferred_element_type=f32)


def _dot_nt(a, b):
    return lax.dot_general(a, b, (((1,), (1,)), ((), ())), preferred_element_type=f32)


def _dot_tn(a, b):
    return lax.dot_general(a, b, (((0,), (0,)), ((), ())), preferred_element_type=f32)


def _split3(x):
    h1 = x.astype(bf16)
    r1 = x - h1.astype(f32)
    h2 = r1.astype(bf16)
    r2 = r1 - h2.astype(f32)
    return h1, h2, r2.astype(bf16)


def _sigmoid(x):
    return 0.5 * (jnp.tanh(0.5 * x) + 1.0)


def _silu(x):
    return x * _sigmoid(x)


def _gelu(x):
    return 0.5 * x * (1.0 + jnp.tanh(math.sqrt(2.0 / math.pi) * (x + 0.044715 * (x * x * x))))


def _softplus(x):
    return jnp.maximum(x, 0.0) + jnp.log1p(jnp.exp(-jnp.abs(x)))


def _layer_norm(x, g, b):
    mu = jnp.mean(x, axis=-1, keepdims=True)
    xc = x - mu
    var = jnp.mean(xc * xc, axis=-1, keepdims=True)
    return xc * lax.rsqrt(var + LN_EPS) * g + b


def _layer_spec(a, l):
    nd = a.ndim
    return pl.BlockSpec((None,) + a.shape[1:], lambda *_: (l,) + (0,) * (nd - 1))


def _params(n_axes=2):
    return pltpu.CompilerParams(dimension_semantics=("arbitrary",) * n_axes, vmem_limit_bytes=VMEM_LIMIT)


def _ada_kernel(c_ref, w_ref, b_ref, o_ref):
    a = _silu(c_ref[...]).astype(bf16)
    o_ref[0] = _dot(a, w_ref[0].astype(bf16)) + b_ref[0]


def _ada_call(c_all, w_ada, b_ada):
    n = c_all.shape[0]
    return pl.pallas_call(
        _ada_kernel,
        grid=(DEPTH, 6),
        in_specs=[
            pl.BlockSpec((n, D_MODEL), lambda l, j: (0, 0)),
            pl.BlockSpec((1, D_MODEL, D_MODEL), lambda l, j: (l, 0, j)),
            pl.BlockSpec((1, 1, D_MODEL), lambda l, j: (l, 0, j)),
        ],
        out_specs=pl.BlockSpec((1, n, D_MODEL), lambda l, j: (l, 0, j)),
        out_shape=jax.ShapeDtypeStruct((DEPTH, n, 6 * D_MODEL), f32),
        compiler_params=_params(),
    )(c_all, w_ada, b_ada.reshape(DEPTH, 1, 6 * D_MODEL))


def _mixer_kernel(x_ref, mod_ref, stc_ref, stp_ref, sts_ref, stm_ref, *rest, bb, lb, start_pos, emit_vn):
    if emit_vn:
        rest = rest[1:]
    (wa_ref, sgw_ref, sgb_ref, slg_ref, slb_ref, cw_ref, cb_ref, clg_ref, clb_ref, pw_ref, ps_ref,
     scw_ref, scb_ref, dtb_ref, alog_ref, dsk_ref, ng_ref, y_ref, nc_ref, np_ref, ns_ref, nm_ref) = rest[:22]
    vn_ref = rest[22] if emit_vn else None
    hbuf, hsh, pbuf, xbuf, sst, cbuf = rest[-6:]
    r = bb * lb
    nb, q = bb, lb
    lq = q.bit_length() - 1
    j = pl.program_id(1)

    @pl.when(j == 0)
    def _():
        hbuf[:, HIST_CONV - (CONV_W - 1):HIST_CONV, :] = stc_ref[...]
        pbuf[:, HIST_POOL - POOL_BUF:HIST_POOL, :] = stp_ref[...]
        xbuf[:, HIST_SCONV - (SSM_CONV - 1):HIST_SCONV, :] = sts_ref[...]
        sst[...] = stm_ref[...]

    x = x_ref[...]
    xm = (x * (1.0 + mod_ref[:, 1:2, :]) + mod_ref[:, 0:1, :]).reshape(r, D_MODEL).astype(bf16)

    row_i = lax.broadcasted_iota(jnp.int32, (r, r), 0)
    col_i = lax.broadcasted_iota(jnp.int32, (r, r), 1)
    same = (row_i >> lq) == (col_i >> lq)
    causal = same & ((col_i & (q - 1)) <= (row_i & (q - 1)))
    causal_t = same & ((row_i & (q - 1)) <= (col_i & (q - 1)))

    ga = _dot(xm, wa_ref[:, OFF_GA:OFF_GA + B_WIDTH])
    gg = _dot(xm, wa_ref[:, OFF_GG:OFF_GG + B_WIDTH])
    h = ga * _sigmoid(gg)
    hbuf[:, HIST_CONV:HIST_CONV + lb, :] = h.reshape(bb, lb, B_WIDTH)
    base = HIST_CONV - (CONV_W - 1)
    span = HIST_CONV + lb - SUBLANES
    for s in range(1, SUBLANES):
        hsh[s - 1, :, 0:span, :] = hbuf[:, s:s + span, :]
    lc = min(lb, 64)
    for rc in range(lb // lc):
        acc = jnp.zeros((bb, lc, B_WIDTH), f32) + cb_ref[...]
        for k in range(CONV_W):
            o = base + k + rc * lc
            s, a = o % SUBLANES, o - o % SUBLANES
            tap = hbuf[:, a:a + lc, :] if s == 0 else hsh[s - 1, :, a:a + lc, :]
            acc = acc + tap * cw_ref[k:k + 1, :]
        cbuf[:, rc * lc:(rc + 1) * lc, :] = acc

    z = _dot(xm, wa_ref[:, OFF_Z:OFF_Z + D_WIDTH])
    xbc = _dot(xm, wa_ref[:, OFF_XBC:OFF_XBC + XBC_WIDTH])
    dtr = _dot(xm, wa_ref[:, OFF_DT:OFF_DT + LANES])
    xbuf[:, HIST_SCONV:HIST_SCONV + lb, :] = xbc.reshape(bb, lb, XBC_WIDTH)
    sbase = HIST_SCONV - (SSM_CONV - 1)
    acc = jnp.zeros((bb, lb, XBC_WIDTH), f32) + scb_ref[...]
    for k in range(SSM_CONV):
        acc = acc + xbuf[:, sbase + k:sbase + k + lb, :] * scw_ref[k:k + 1, :]
    xc = _silu(acc).reshape(r, XBC_WIDTH)
    new_s = xbuf[:, lb + sbase:lb + HIST_SCONV, :]
    ns_ref[...] = new_s
    xbuf[:, sbase:HIST_SCONV, :] = new_s

    u = _gelu(_dot(xm, wa_ref[:, OFF_U:OFF_U + A_WIDTH]))
    v = _gelu(_dot(xm, wa_ref[:, OFF_V:OFF_V + A_WIDTH]))
    pin = _dot(xm, wa_ref[:, OFF_P:OFF_P + C_WIDTH])
    vn = _layer_norm(v, slg_ref[...], slb_ref[...])
    if emit_vn:
        vn_ref[...] = vn.reshape(bb, lb, A_WIDTH)
    vnb = vn.astype(bf16)
    for g in range(A_GROUPS):
        sl = slice(g * LANES, (g + 1) * LANES)
        s = _dot(sgw_ref[g], vnb[:, sl]) + sgb_ref[:, sl]
        y_ref[:, sl] = (u[:, sl] * s).astype(bf16)

    yb = _silu(_layer_norm(cbuf[...].reshape(r, B_WIDTH), clg_ref[...], clb_ref[...]))
    y_ref[:, A_WIDTH:A_WIDTH + B_WIDTH] = yb.astype(bf16)
    new_c = hbuf[:, lb + base:lb + HIST_CONV, :]
    nc_ref[...] = new_c
    hbuf[:, base:HIST_CONV, :] = new_c

    pbuf[:, HIST_POOL:HIST_POOL + lb, :] = pin.reshape(bb, lb, C_WIDTH)
    pos = (start_pos + j * lb + lax.broadcasted_iota(jnp.int32, (bb, lb, LANES), 1)).astype(f32)
    for gi, win in enumerate(POOL_WINDOWS):
        sl = slice(gi * LANES, (gi + 1) * LANES)
        acc = pbuf[:, HIST_POOL:HIST_POOL + lb, sl]
        for i in range(1, win):
            acc = acc + pbuf[:, HIST_POOL - i:HIST_POOL - i + lb, sl]
        cnt = jnp.minimum(float(win), pos + 1.0)
        mixed = (acc / cnt - pbuf[:, HIST_POOL:HIST_POOL + lb, sl]).reshape(r, LANES)
        out = _dot(mixed.astype(bf16), pw_ref[gi].astype(bf16)) * ps_ref[:, sl]
        y_ref[:, A_WIDTH + B_WIDTH + gi * LANES:A_WIDTH + B_WIDTH + (gi + 1) * LANES] = out.astype(bf16)
    new_p = pbuf[:, lb + HIST_POOL - POOL_BUF:lb + HIST_POOL, :]
    np_ref[...] = new_p
    pbuf[:, HIST_POOL - POOL_BUF:HIST_POOL, :] = new_p

    dt = _softplus(dtr + dtb_ref[...])
    da = dt * (-jnp.exp(alog_ref[...]))
    d1, d2, d3 = _split3(da)
    mc = jnp.where(causal, 1.0, 0.0).astype(bf16)
    mct = jnp.where(causal_t, 1.0, 0.0).astype(bf16)
    ms = jnp.where(same, 1.0, 0.0).astype(bf16)
    acs = _dot(mc, d1) + _dot(mc, d2) + _dot(mc, d3)
    acs_t = _dot_tn(d1, mct) + _dot_tn(d2, mct) + _dot_tn(d3, mct)
    tot = _dot(ms, d1) + _dot(ms, d2) + _dot(ms, d3)

    lane = lax.broadcasted_iota(jnp.int32, (r, LANES), 1)
    lo_half = lane < SSM_HEADDIM
    cbs = []
    for g in range(SSM_GROUPS):
        bg = xc[:, D_WIDTH + g * SSM_STATE:D_WIDTH + (g + 1) * SSM_STATE].astype(bf16)
        cg = xc[:, D_WIDTH + (SSM_GROUPS + g) * SSM_STATE:D_WIDTH + (SSM_GROUPS + g + 1) * SSM_STATE].astype(bf16)
        cbs.append((bg, cg, _dot_nt(cg, bg)))
    rep = SSM_HEADS // SSM_GROUPS
    rowseq = lax.broadcasted_iota(jnp.int32, (r, LANES), 0) >> lq
    ysq = jnp.zeros((r, 1), f32)
    ypairs = []
    for i in range(SSM_HEADS // 2):
        h0, h1 = 2 * i, 2 * i + 1
        g0, g1 = h0 // rep, h1 // rep
        sl = slice(i * LANES, (i + 1) * LANES)
        xs = xc[:, sl]
        dtp = jnp.where(lo_half, dt[:, h0:h0 + 1], dt[:, h1:h1 + 1])
        acp = jnp.where(lo_half, acs[:, h0:h0 + 1], acs[:, h1:h1 + 1])
        totp = jnp.where(lo_half, tot[:, h0:h0 + 1], tot[:, h1:h1 + 1])
        xd = xs * dtp
        xdb = xd.astype(bf16)
        res = []
        for hh, gg_ in ((h0, g0), (h1, g1)):
            diff = acs[:, hh:hh + 1] - acs_t[hh:hh + 1, :]
            dec = jnp.exp(jnp.where(causal, diff, -1e30))
            res.append(_dot((cbs[gg_][2] * dec).astype(bf16), xdb))
        ydiag = jnp.where(lo_half, res[0], res[1])
        s_all = sst[:, i * LANES:(i + 1) * LANES, :].reshape(nb * LANES, SSM_STATE)
        s_bf = s_all.astype(bf16)
        full0 = _dot_nt(cbs[g0][1], s_bf)
        full1 = full0 if g1 == g0 else _dot_nt(cbs[g1][1], s_bf)
        yoff = jnp.zeros((r, LANES), f32)
        for b in range(nb):
            blk = jnp.where(lo_half, full0[:, b * LANES:(b + 1) * LANES], full1[:, b * LANES:(b + 1) * LANES])
            yoff = yoff + (jnp.where(rowseq == b, blk, 0.0) if nb > 1 else blk)
        yp = ydiag + yoff * jnp.exp(acp) + dsk_ref[:, sl] * xs
        yp = yp * _silu(z[:, sl])
        ysq = ysq + jnp.sum(yp * yp, axis=-1, keepdims=True)
        ypairs.append(yp)
        xdd = xd * jnp.exp(totp - acp)
        if nb > 1:
            colblk = lax.broadcasted_iota(jnp.int32, (r, nb * LANES), 1) >> 7
            rowblk = lax.broadcasted_iota(jnp.int32, (r, nb * LANES), 0) >> lq
            xblk = jnp.where(colblk == rowblk, jnp.concatenate([xdd] * nb, axis=1), 0.0).astype(bf16)
        else:
            xblk = xdd.astype(bf16)
        upd0 = _dot_tn(xblk, cbs[g0][0])
        if g1 != g0:
            upd1 = _dot_tn(xblk, cbs[g1][0])
            prow = lax.broadcasted_iota(jnp.int32, (nb * LANES, SSM_STATE), 0) & (LANES - 1)
            upd0 = jnp.where(prow < SSM_HEADDIM, upd0, upd1)
        upd0 = upd0.reshape(nb, LANES, SSM_STATE)
        for b in range(nb):
            for hh, half in ((h0, 0), (h1, 1)):
                fac = jnp.exp(jnp.broadcast_to(tot[b * q:b * q + 1, hh:hh + 1], (SSM_HEADDIM, SSM_STATE)))
                rs = slice(i * LANES + half * SSM_HEADDIM, i * LANES + (half + 1) * SSM_HEADDIM)
                sst[b, rs, :] = fac * sst[b, rs, :] + upd0[b, half * SSM_HEADDIM:(half + 1) * SSM_HEADDIM, :]
    rms = lax.rsqrt(ysq * (1.0 / D_WIDTH) + LN_EPS)
    for i in range(SSM_HEADS // 2):
        sl = slice(i * LANES, (i + 1) * LANES)
        yd = ypairs[i] * rms * ng_ref[:, sl]
        o = A_WIDTH + B_WIDTH + C_WIDTH + i * LANES
        y_ref[:, o:o + LANES] = yd.astype(bf16)
    nm_ref[...] = sst[...]


def _merge_kernel(x_ref, mod_ref, y_ref, wg_ref, wbr_ref, wo_ref, lg_ref, lb_ref, wr_ref,
                  x1_ref, xpk_ref, rt_ref, *, bb, lb):
    r = bb * lb
    x = x_ref[...]
    xm = (x * (1.0 + mod_ref[:, 1:2, :]) + mod_ref[:, 0:1, :]).reshape(r, D_MODEL).astype(bf16)
    y = y_ref[...]
    offs = (0, A_WIDTH, A_WIDTH + B_WIDTH, A_WIDTH + B_WIDTH + C_WIDTH, Y_WIDTH)
    merged = jnp.zeros((r, D_MODEL), f32)
    for k in range(N_BRANCH):
        gate = _sigmoid(_dot(xm, wg_ref[:, k * D_MODEL:(k + 1) * D_MODEL]))
        merged = merged + gate * _dot(y[:, offs[k]:offs[k + 1]], wbr_ref[offs[k]:offs[k + 1], :])
    mix = _dot(merged.astype(bf16), wo_ref[...]).reshape(bb, lb, D_MODEL)
    x1 = _layer_norm(DN_ALPHA * x + mod_ref[:, 2:3, :] * mix, lg_ref[...], lb_ref[...])
    x1_ref[...] = x1
    xm2 = (x1 * (1.0 + mod_ref[:, 4:5, :]) + mod_ref[:, 3:4, :]).reshape(r, D_MODEL)
    xpk_ref[...] = _pack_bf16_pair(xm2[:, :HALF], xm2[:, HALF:])
    rt_ref[...] = _route(xm2, wr_ref)


def _route(xm2, wr_ref):
    r = xm2.shape[0]
    a1, a2, _ = _split3(xm2)
    lg = _dot(a1, wr_ref[0]) + (_dot(a2, wr_ref[0]) + _dot(a1, wr_ref[1]))
    lane = lax.broadcasted_iota(jnp.int32, (r, LANES), 1).astype(f32)
    neg, big = -1e30, 1e6
    mask_g = lane < 4.0
    lgg = jnp.where(mask_g, lg, neg)
    mg = jnp.max(lgg, axis=-1, keepdims=True)
    grp = jnp.min(jnp.where(lgg == mg, lane, big), axis=-1, keepdims=True)
    p_grp = 1.0 / jnp.sum(jnp.where(mask_g, jnp.exp(lgg - mg), 0.0), axis=-1, keepdims=True)
    lo = 4.0 + grp * E_PER_GROUP
    mask_e = (lane >= lo) & (lane < lo + E_PER_GROUP)
    lge = jnp.where(mask_e, lg, neg)
    v1 = jnp.max(lge, axis=-1, keepdims=True)
    i1 = jnp.min(jnp.where(lge == v1, lane, big), axis=-1, keepdims=True)
    lge2 = jnp.where(lane == i1, neg, lge)
    v2 = jnp.max(lge2, axis=-1, keepdims=True)
    i2 = jnp.min(jnp.where(lge2 == v2, lane, big), axis=-1, keepdims=True)
    e21 = jnp.exp(v2 - v1)
    w1 = p_grp / (1.0 + e21)
    w2 = p_grp * e21 / (1.0 + e21)
    return jnp.where(lane == 0.0, i1 - 4.0,
                     jnp.where(lane == 1.0, i2 - 4.0,
                               jnp.where(lane == 2.0, w1, jnp.where(lane == 3.0, w2, 0.0))))


MIXER_WEIGHTS = ("wa", "sgu_w", "sgu_bias_rows", "sgu_ln_g", "sgu_ln_b", "conv_w", "conv_bias", "conv_ln_g",
                 "conv_ln_b", "pool_w", "pool_scale", "ssm_conv_w", "ssm_conv_b", "dt_bias", "a_log", "d_skip",
                 "norm_g")
MERGE_WEIGHTS = ("w_gl", "w_br", "w_o", "ln1_g", "ln1_b", "w_router")
BIG_WEIGHTS = ("wa", "w_gl", "w_br", "w_o")
STATE_SHAPES = ((CONV_W - 1, B_WIDTH), (POOL_BUF, C_WIDTH), (SSM_CONV - 1, XBC_WIDTH), (D_WIDTH, SSM_STATE))


def _weight_spec(w, name, l):
    spec = _layer_spec(w[name], l)
    if name in BIG_WEIGHTS:
        spec = pl.BlockSpec(spec.block_shape, spec.index_map, pipeline_mode=pl.Buffered(1))
    return spec


def _mixer_call(x, ada, states, vn_buf, w, l, *, bb, lb, start_pos, mod_off):
    b, seq, _ = x.shape
    r = bb * lb
    nl = seq // lb
    emit_vn = vn_buf is not None
    kern = functools.partial(_mixer_kernel, bb=bb, lb=lb, start_pos=start_pos, emit_vn=emit_vn)

    def state_spec(shape):
        return pl.BlockSpec((None, bb) + shape, lambda i, j: (l, i) + (0,) * len(shape))

    weights = [w[n] for n in MIXER_WEIGHTS]
    extra_in = [vn_buf] if emit_vn else []
    in_specs = ([pl.BlockSpec((bb, lb, D_MODEL), lambda i, j: (i, j, 0)),
                 pl.BlockSpec((None, bb, 6, D_MODEL), lambda i, j: (l, mod_off // bb + i, 0, 0))]
                + [state_spec(s) for s in STATE_SHAPES]
                + [pl.BlockSpec(memory_space=pl.ANY) for _ in extra_in]
                + [_weight_spec(w, n, l) for n in MIXER_WEIGHTS])
    n_st = len(STATE_SHAPES)
    out_shape = [jax.ShapeDtypeStruct((b * seq, Y_WIDTH), bf16)]
    out_shape += [jax.ShapeDtypeStruct(s.shape, f32) for s in states]
    out_specs = [pl.BlockSpec((r, Y_WIDTH), lambda i, j: (i * nl + j, 0))] + [state_spec(s) for s in STATE_SHAPES]
    aliases = {2 + k: 1 + k for k in range(n_st)}
    if emit_vn:
        out_shape.append(jax.ShapeDtypeStruct(vn_buf.shape, f32))
        out_specs.append(pl.BlockSpec((None, bb, lb, A_WIDTH), lambda i, j: (l, i, j, 0)))
        aliases[2 + n_st] = 1 + n_st
    scratch = [
        pltpu.VMEM((bb, HIST_CONV + lb, B_WIDTH), f32),
        pltpu.VMEM((SUBLANES - 1, bb, HIST_CONV + lb, B_WIDTH), f32),
        pltpu.VMEM((bb, HIST_POOL + lb, C_WIDTH), f32),
        pltpu.VMEM((bb, HIST_SCONV + lb, XBC_WIDTH), f32),
        pltpu.VMEM((bb, D_WIDTH, SSM_STATE), f32),
        pltpu.VMEM((bb, lb, B_WIDTH), f32),
    ]
    res = pl.pallas_call(
        kern, grid=(b // bb, nl), in_specs=in_specs, out_specs=out_specs, out_shape=out_shape,
        scratch_shapes=scratch, input_output_aliases=aliases, compiler_params=_params(),
    )(x, ada, *states, *extra_in, *weights)
    return res[0], tuple(res[1:1 + n_st]), (res[1 + n_st] if emit_vn else None)


def _merge_call(x, ada, ycat, w, l, *, bb, lb, mod_off):
    b, seq, _ = x.shape
    r = bb * lb
    nl = seq // lb
    n = b * seq
    weights = [w[k] for k in MERGE_WEIGHTS]
    return pl.pallas_call(
        functools.partial(_merge_kernel, bb=bb, lb=lb), grid=(b // bb, nl),
        in_specs=[pl.BlockSpec((bb, lb, D_MODEL), lambda i, j: (i, j, 0)),
                  pl.BlockSpec((None, bb, 6, D_MODEL), lambda i, j: (l, mod_off // bb + i, 0, 0)),
                  pl.BlockSpec((r, Y_WIDTH), lambda i, j: (i * nl + j, 0))]
        + [_weight_spec(w, k, l) for k in MERGE_WEIGHTS],
        out_specs=(pl.BlockSpec((bb, lb, D_MODEL), lambda i, j: (i, j, 0)),
                   pl.BlockSpec((r, HALF), lambda i, j: (i * nl + j, 0)),
                   pl.BlockSpec((r, LANES), lambda i, j: (i * nl + j, 0))),
        out_shape=(jax.ShapeDtypeStruct((b, seq, D_MODEL), f32),
                   jax.ShapeDtypeStruct((n, HALF), jnp.uint32),
                   jax.ShapeDtypeStruct((n, LANES), f32)),
        compiler_params=_params(),
    )(x, ada, ycat, *weights)


def _pack_bf16_pair(a, b):
    ua = lax.bitcast_convert_type(a.astype(bf16).astype(f32), jnp.uint32)
    ub = lax.bitcast_convert_type(b.astype(bf16).astype(f32), jnp.uint32)
    return (ua & jnp.uint32(0xFFFF0000)) | (ub >> 16)


def _unpack_bf16_pair(u):
    hi = lax.bitcast_convert_type(u & jnp.uint32(0xFFFF0000), f32)
    lo = lax.bitcast_convert_type(u << 16, f32)
    return hi, lo


def _rank_kernel(rt_ref, dest_ref, cnt_ref, carry, start, *, t, n_tiles):
    p = pl.program_id(1)
    j = pl.program_id(2)
    rt = rt_ref[...]
    lane = lax.broadcasted_iota(jnp.int32, (t, LANES), 1).astype(f32)
    oh0 = jnp.where(lane == rt[:, 0:1], 1.0, 0.0)
    oh1 = jnp.where(lane == rt[:, 1:2], 1.0, 0.0)
    both = oh0 + oh1
    colsum = jnp.sum(both, axis=0, keepdims=True)

    @pl.when((p == 0) & (j == 0))
    def _():
        carry[...] = jnp.zeros(carry.shape, f32)

    @pl.when(p == 0)
    def _():
        carry[0:1, :] = carry[0:1, :] + colsum

    @pl.when((p == 0) & (j == n_tiles - 1))
    def _():
        c = jnp.broadcast_to(carry[0:1, :], (SUBLANES, LANES))
        cnt_ref[0] = c
        nblk = jnp.floor((c + (MOE_BLOCK - 1.0)) * (1.0 / MOE_BLOCK))
        ri = lax.broadcasted_iota(jnp.int32, (LANES, LANES), 0)
        ci = lax.broadcasted_iota(jnp.int32, (LANES, LANES), 1)
        tri = jnp.where(ri < ci, 1.0, 0.0).astype(bf16)
        start[...] = _dot(nblk.astype(bf16), tri) * float(MOE_BLOCK)
        carry[...] = jnp.zeros(carry.shape, f32)

    @pl.when(p == 1)
    def _():
        rr = lax.broadcasted_iota(jnp.int32, (t, t), 0)
        cc = lax.broadcasted_iota(jnp.int32, (t, t), 1)
        ltri = jnp.where(cc < rr, 1.0, 0.0).astype(bf16)
        base = start[0:1, :] + carry[0:1, :] + _dot(ltri, both.astype(bf16))
        d0 = jnp.sum(oh0 * base, axis=-1, keepdims=True)
        d1 = jnp.sum(oh1 * base, axis=-1, keepdims=True)
        arr = jnp.where(lane == 0.0, d0, jnp.where(lane == 1.0, d1, 0.0))
        dest_ref[0] = arr.T[0:2, :].astype(jnp.int32)
        carry[0:1, :] = carry[0:1, :] + colsum


def _rank_call(route, *, n_groups, gsz):
    t = min(RANK_TILE, gsz)
    n_tiles = gsz // t
    return pl.pallas_call(
        functools.partial(_rank_kernel, t=t, n_tiles=n_tiles),
        grid=(n_groups, 2, n_tiles),
        in_specs=[pl.BlockSpec((t, LANES), lambda g, p, j: (g * n_tiles + j, 0))],
        out_specs=(pl.BlockSpec((1, 2, t), lambda g, p, j: (g, 0, j * p)),
                   pl.BlockSpec((1, SUBLANES, LANES), lambda g, p, j: (g, 0, 0))),
        out_shape=(jax.ShapeDtypeStruct((n_groups, 2, gsz), jnp.int32),
                   jax.ShapeDtypeStruct((n_groups, SUBLANES, LANES), f32)),
        scratch_shapes=[pltpu.VMEM((SUBLANES, LANES), f32), pltpu.VMEM((SUBLANES, LANES), f32)],
        compiler_params=_params(3),
    )(route)


def _expert_kernel(be_ref, nv_ref, dest_ref, x_ref, wg0_ref, wu0_ref, wd0_ref, wg1_ref, wu1_ref, wd1_ref,
                   o_ref, xg, yg, tbl, *, gsz, max_blocks):
    g = pl.program_id(0)
    b = pl.program_id(1)
    pad_row = 2 * gsz

    nv = nv_ref[g]
    xgs = tuple(xg.at[k] for k in range(4))
    ygs = tuple(yg.at[k] for k in range(4))
    w_refs = ((wg0_ref, wu0_ref, wd0_ref), (wg1_ref, wu1_ref, wd1_ref))

    @pl.when(b == 0)
    def _():
        def init(i, c):
            tbl[i] = pad_row
            return c

        lax.fori_loop(0, max_blocks * MOE_BLOCK, init, 0, unroll=8)

        def fill(t, c):
            tbl[dest_ref[0, 0, t]] = t
            tbl[dest_ref[0, 1, t]] = gsz + t
            return c

        lax.fori_loop(0, gsz, fill, 0, unroll=8)
        o_ref[0, pad_row:pad_row + SUBLANES, :] = jnp.zeros((SUBLANES, HALF), jnp.uint32)
        yg[...] = jnp.zeros(yg.shape, jnp.uint32)

        def gather0(i, c):
            xg[0, pl.ds(i, 1), :] = x_ref[pl.ds(tbl[i] & (gsz - 1), 1), :]
            second = jnp.minimum(1, nv - 1) * MOE_BLOCK
            xg[1, pl.ds(i, 1), :] = x_ref[pl.ds(tbl[second + i] & (gsz - 1), 1), :]
            return c

        lax.fori_loop(0, MOE_BLOCK, gather0, 0, unroll=8)

    def step(par):
        nxt = 1 - par
        for t in range(2):
            kn = 2 * (b + 1) + t
            kp = 2 * (b - 1) + t
            base_n = jnp.minimum(kn, nv - 1) * MOE_BLOCK
            base_p = jnp.clip(kp, 0, nv - 1) * MOE_BLOCK
            live = (kp >= 0) & (kp < nv)
            x_nxt, y_prv = xgs[2 * nxt + t], ygs[2 * nxt + t]
            for i in range(MOE_BLOCK):
                x_nxt[i:i + 1, :] = x_ref[pl.ds(tbl[base_n + i] & (gsz - 1), 1), :]
                s = jnp.where(live, tbl[base_p + i], pad_row)
                o_ref[0, pl.ds(s, 1), :] = y_prv[i:i + 1, :]
        for t in range(2):
            wg_ref, wu_ref, wd_ref = w_refs[t]
            hi, lo = _unpack_bf16_pair(xgs[2 * par + t][...])
            hi = hi.astype(bf16)
            lo = lo.astype(bf16)
            hg = _dot(hi, wg_ref[:HALF, :]) + _dot(lo, wg_ref[HALF:, :])
            hu = _dot(hi, wu_ref[:HALF, :]) + _dot(lo, wu_ref[HALF:, :])
            hid = (_silu(hg) * hu).astype(bf16)
            y = _dot(hid, wd_ref[...])
            ygs[2 * par + t][...] = _pack_bf16_pair(y[:, :HALF], y[:, HALF:])

    @pl.when((2 * b <= nv + 1) & (b % 2 == 0))
    def _():
        step(0)

    @pl.when((2 * b <= nv + 1) & (b % 2 == 1))
    def _():
        step(1)


def _expert_call(xpk, block_e, n_valid, dest, w, l, *, n_groups, gsz, max_blocks):
    assert gsz & (gsz - 1) == 0

    def w_spec(shape, t):
        return pl.BlockSpec(
            (None, None) + shape,
            lambda g, b, be, nv: (l, be[g * max_blocks + jnp.minimum(2 * b + t, max_blocks - 1)], 0, 0))

    single = pl.Buffered(1)
    w_specs = [w_spec(s, t) for t in range(2)
               for s in ((D_MODEL, EXPERT_FF), (D_MODEL, EXPERT_FF), (EXPERT_FF, D_MODEL))]
    grid_spec = pltpu.PrefetchScalarGridSpec(
        num_scalar_prefetch=2,
        grid=(n_groups, (max_blocks + 1) // 2 + 1),
        in_specs=[
            pl.BlockSpec((1, 2, gsz), lambda g, b, be, nv: (g, 0, 0), memory_space=pltpu.SMEM),
            pl.BlockSpec((gsz, HALF), lambda g, b, be, nv: (g, 0), pipeline_mode=single),
        ] + w_specs,
        out_specs=pl.BlockSpec((1, 2 * gsz + SUBLANES, HALF), lambda g, b, be, nv: (g, 0, 0),
                               pipeline_mode=single),
        scratch_shapes=[pltpu.VMEM((4, MOE_BLOCK, HALF), jnp.uint32), pltpu.VMEM((4, MOE_BLOCK, HALF), jnp.uint32),
                        pltpu.SMEM((max_blocks * MOE_BLOCK,), jnp.int32)],
    )
    ws = (w["w_e_gate"], w["w_e_up"], w["w_e_down"])
    return pl.pallas_call(
        functools.partial(_expert_kernel, gsz=gsz, max_blocks=max_blocks), grid_spec=grid_spec,
        out_shape=jax.ShapeDtypeStruct((n_groups, 2 * gsz + SUBLANES, HALF), jnp.uint32),
        compiler_params=_params(),
    )(block_e, n_valid, dest, xpk, *ws, *ws)


def _combine_kernel(x_ref, mod_ref, y0_ref, y1_ref, rt_ref, lg_ref, lb_ref, o_ref, *, bb, lb):
    rt = rt_ref[...]
    w0 = rt[:, 2:3]
    w1 = rt[:, 3:4]
    h0, l0 = _unpack_bf16_pair(y0_ref[0])
    h1, l1 = _unpack_bf16_pair(y1_ref[0])
    f = jnp.concatenate([w0 * h0 + w1 * h1, w0 * l0 + w1 * l1], axis=1).reshape(bb, lb, D_MODEL)
    t = DN_ALPHA * x_ref[...] + mod_ref[:, 5:6, :] * f
    o_ref[...] = _layer_norm(t, lg_ref[...], lb_ref[...])


def _combine_call(x1, ada, ys, route, w, l, *, bb, lb, gsz, mod_off):
    b, seq, _ = x1.shape
    r = bb * lb
    nl = seq // lb
    tiles_per_group = gsz // r

    def y_map(k):
        def m(i, j):
            t = i * nl + j
            return (t // tiles_per_group, k * tiles_per_group + t % tiles_per_group, 0)
        return m

    return pl.pallas_call(
        functools.partial(_combine_kernel, bb=bb, lb=lb),
        grid=(b // bb, nl),
        in_specs=[pl.BlockSpec((bb, lb, D_MODEL), lambda i, j: (i, j, 0)),
                  pl.BlockSpec((None, bb, 6, D_MODEL), lambda i, j: (l, mod_off // bb + i, 0, 0)),
                  pl.BlockSpec((1, r, HALF), y_map(0)),
                  pl.BlockSpec((1, r, HALF), y_map(1)),
                  pl.BlockSpec((r, LANES), lambda i, j: (i * nl + j, 0)),
                  _layer_spec(w["ln2_g"], l), _layer_spec(w["ln2_b"], l)],
        out_specs=pl.BlockSpec((bb, lb, D_MODEL), lambda i, j: (i, j, 0)),
        out_shape=jax.ShapeDtypeStruct((b, seq, D_MODEL), f32),
        compiler_params=_params(),
    )(x1, ada, ys, ys, route, w["ln2_g"], w["ln2_b"])


def _moe(x1, ada, xpk, route, w, l, *, bb, lb, gsz, mod_off):
    n = xpk.shape[0]
    n_groups = n // gsz
    max_blocks = (2 * gsz + N_EXPERTS * (MOE_BLOCK - 1)) // MOE_BLOCK
    dest, counts = _rank_call(route, n_groups=n_groups, gsz=gsz)
    cnt = counts[:, 0, :N_EXPERTS].astype(jnp.int32)
    end_blk = jnp.cumsum((cnt + MOE_BLOCK - 1) // MOE_BLOCK, axis=-1)
    n_valid = end_blk[:, -1]
    blk = jnp.arange(max_blocks, dtype=jnp.int32)
    block_e = jnp.sum(blk[None, :, None] >= end_blk[:, None, :], axis=-1).astype(jnp.int32)
    block_e = jnp.minimum(block_e, N_EXPERTS - 1).reshape(n_groups * max_blocks)
    ys = _expert_call(xpk, block_e, n_valid, dest, w, l, n_groups=n_groups, gsz=gsz, max_blocks=max_blocks)
    return _combine_call(x1, ada, ys, route, w, l, bb=bb, lb=lb, gsz=gsz, mod_off=mod_off)


def _win_kernel(w_ref, wa_ref, wgl_ref):
    wa_ref[0] = w_ref[0, :, 0:WA_WIDTH].astype(bf16)
    wgl_ref[0] = w_ref[0, :, GATE_OFF:GATE_OFF + N_BRANCH * D_MODEL].astype(bf16)


def _win_call(w_in):
    rows = 128
    width = w_in.shape[-1]
    return pl.pallas_call(
        _win_kernel, grid=(DEPTH, D_MODEL // rows),
        in_specs=[pl.BlockSpec((1, rows, width), lambda l, i: (l, i, 0))],
        out_specs=(pl.BlockSpec((1, rows, WA_WIDTH), lambda l, i: (l, i, 0)),
                   pl.BlockSpec((1, rows, N_BRANCH * D_MODEL), lambda l, i: (l, i, 0))),
        out_shape=(jax.ShapeDtypeStruct((DEPTH, D_MODEL, WA_WIDTH), bf16),
                   jax.ShapeDtypeStruct((DEPTH, D_MODEL, N_BRANCH * D_MODEL), bf16)),
        compiler_params=_params(),
    )(w_in)


def _prep_weights(p):
    wa, w_gl = _win_call(p["w_in"])
    lane_pad = lambda v: jnp.pad(v, ((0, 0), (0, LANES - v.shape[1])))[:, None, :]
    row = lambda v: v[:, None, :]
    wr = jnp.pad(jnp.concatenate([p["router_g"], p["router_e"]], axis=-1),
                 ((0, 0), (0, 0), (0, LANES - 4 - N_EXPERTS)))
    wr_hi = wr.astype(bf16)
    wr_lo = (wr - wr_hi.astype(f32)).astype(bf16)
    return dict(
        wa=wa, w_gl=w_gl,
        sgu_ln_g=row(p["sgu_ln_g"]), sgu_ln_b=row(p["sgu_ln_b"]),
        conv_w=p["conv_w"], conv_bias=row(p["conv_bias"]),
        conv_ln_g=row(p["conv_ln_g"]), conv_ln_b=row(p["conv_ln_b"]),
        pool_w=p["pool_w"], pool_scale=row(p["pool_scale"]),
        ssm_conv_w=p["ssm_conv_w"], ssm_conv_b=row(p["ssm_conv_b"]),
        dt_bias=lane_pad(p["ssm_dt_bias"]), a_log=lane_pad(p["ssm_a_log"]),
        d_skip=row(jnp.repeat(p["ssm_d"], SSM_HEADDIM, axis=-1)), norm_g=row(p["ssm_norm_g"]),
        w_br=jnp.concatenate([p["w_br_a"], p["w_br_b"], p["w_br_c"], p["w_br_d"]], axis=1).astype(bf16),
        w_o=p["w_o"].astype(bf16),
        ln1_g=row(p["ln1_g"]), ln1_b=row(p["ln1_b"]),
        w_router=jnp.stack([wr_hi, wr_lo], axis=1),
        w_e_gate=p["w_e_gate"].astype(bf16), w_e_up=p["w_e_up"].astype(bf16), w_e_down=p["w_e_down"].astype(bf16),
        ln2_g=row(p["ln2_g"]), ln2_b=row(p["ln2_b"]),
    )


def _prep_sgu(p, q, r):
    nb = r // q
    idx = np.arange(r)
    keep = (idx[:, None] // q == idx[None, :] // q) & (idx[None, :] % q <= idx[:, None] % q)
    sgu_w = jnp.tile(p["sgu_w"][:, :, :q, :q], (1, 1, nb, nb))
    sgu_w = jnp.where(keep, sgu_w, 0.0).astype(bf16)
    bias = jnp.tile(p["sgu_b"][:, :, :q], (1, 1, nb))
    bias = jnp.repeat(jnp.swapaxes(bias, 1, 2), LANES, axis=-1)
    return dict(sgu_w=sgu_w, sgu_bias_rows=bias)


PROMPT_MIX = (1, 128)
SAMPLE_MIX = (8, 8)
PROMPT_TOK = (1, 256)
SAMPLE_TOK = (32, 8)
PROMPT_GROUP = 4096
SAMPLE_GROUP = 1024


def kernel(x_prompt, x_sample, state_conv, state_pool, state_ssm_conv, state_ssm, c_prompt, c_sample,
           w_ada, b_ada, w_in, sgu_ln_g, sgu_ln_b, sgu_w, sgu_b, conv_w, conv_bias, conv_ln_g, conv_ln_b,
           pool_w, pool_scale, ssm_conv_w, ssm_conv_b, ssm_dt_bias, ssm_a_log, ssm_d, ssm_norm_g,
           w_br_a, w_br_b, w_br_c, w_br_d, w_o, ln1_g, ln1_b, router_g, router_e, w_e_gate, w_e_up,
           w_e_down, ln2_g, ln2_b):
    return _forward(x_prompt, x_sample, state_conv, state_pool, state_ssm_conv, state_ssm, c_prompt, c_sample,
                    w_ada, b_ada, w_in, sgu_ln_g, sgu_ln_b, sgu_w, sgu_b, conv_w, conv_bias, conv_ln_g, conv_ln_b,
                    pool_w, pool_scale, ssm_conv_w, ssm_conv_b, ssm_dt_bias, ssm_a_log, ssm_d, ssm_norm_g,
                    w_br_a, w_br_b, w_br_c, w_br_d, w_o, ln1_g, ln1_b, router_g, router_e, w_e_gate, w_e_up,
                    w_e_down, ln2_g, ln2_b)


def _forward(x_prompt, x_sample, state_conv, state_pool, state_ssm_conv, state_ssm, c_prompt, c_sample,
             w_ada, b_ada, w_in, sgu_ln_g, sgu_ln_b, sgu_w, sgu_b, conv_w, conv_bias, conv_ln_g, conv_ln_b,
             pool_w, pool_scale, ssm_conv_w, ssm_conv_b, ssm_dt_bias, ssm_a_log, ssm_d, ssm_norm_g,
             w_br_a, w_br_b, w_br_c, w_br_d, w_o, ln1_g, ln1_b, router_g, router_e, w_e_gate, w_e_up,
             w_e_down, ln2_g, ln2_b, prompt_group=PROMPT_GROUP, sample_group=SAMPLE_GROUP,
             sample_tok=SAMPLE_TOK):
    p = dict(w_in=w_in, sgu_ln_g=sgu_ln_g, sgu_ln_b=sgu_ln_b, sgu_w=sgu_w, sgu_b=sgu_b, conv_w=conv_w,
             conv_bias=conv_bias, conv_ln_g=conv_ln_g, conv_ln_b=conv_ln_b, pool_w=pool_w, pool_scale=pool_scale,
             ssm_conv_w=ssm_conv_w, ssm_conv_b=ssm_conv_b, ssm_dt_bias=ssm_dt_bias, ssm_a_log=ssm_a_log,
             ssm_d=ssm_d, ssm_norm_g=ssm_norm_g, w_br_a=w_br_a, w_br_b=w_br_b, w_br_c=w_br_c, w_br_d=w_br_d,
             w_o=w_o, ln1_g=ln1_g, ln1_b=ln1_b, router_g=router_g, router_e=router_e, ln2_g=ln2_g, ln2_b=ln2_b,
             w_e_gate=w_e_gate, w_e_up=w_e_up, w_e_down=w_e_down)
    bp = x_prompt.shape[0]
    bs = x_sample.shape[0]
    w = _prep_weights(p)
    ada = _ada_call(jnp.concatenate([c_sample, c_prompt], axis=0), w_ada, b_ada)
    ada = ada.reshape(DEPTH, bs + bp, 6, D_MODEL)

    groups = [
        dict(x=x_prompt, mix=PROMPT_MIX, tok=PROMPT_TOK, gsz=prompt_group, start=0, mod_off=bs, vn=None,
             states=tuple(jnp.zeros((DEPTH, bp) + s, f32) for s in STATE_SHAPES)),
        dict(x=x_sample, mix=SAMPLE_MIX, tok=sample_tok, gsz=sample_group, start=PAST_LEN, mod_off=0,
             vn=jnp.zeros((DEPTH, bs, x_sample.shape[1], A_WIDTH), f32),
             states=(state_conv, state_pool, state_ssm_conv, state_ssm.reshape(DEPTH, bs, D_WIDTH, SSM_STATE))),
    ]
    for g in groups:
        g["w"] = dict(w, **_prep_sgu(p, g["mix"][1], g["mix"][0] * g["mix"][1]))
    for l in range(DEPTH):
        for g in groups:
            bb, lb = g["mix"]
            ycat, g["states"], g["vn"] = _mixer_call(
                g["x"], ada, g["states"], g["vn"], g["w"], l, bb=bb, lb=lb, start_pos=g["start"],
                mod_off=g["mod_off"])
            bb, lb = g["tok"]
            x1, xpk, route = _merge_call(g["x"], ada, ycat, g["w"], l, bb=bb, lb=lb, mod_off=g["mod_off"])
            g["x"] = _moe(x1, ada, xpk, route, g["w"], l, bb=bb, lb=lb, gsz=g["gsz"], mod_off=g["mod_off"])

    def states_out(g, b):
        c, pl_, sc, sm = g["states"]
        return c, pl_, sc, sm.reshape(DEPTH, b, SSM_HEADS, SSM_HEADDIM, SSM_STATE)

    return ((groups[0]["x"], groups[1]["x"]) + states_out(groups[0], bp) + states_out(groups[1], bs)
            + (groups[1]["vn"],))
```
